```python
import math
import jax
import jax.numpy as jnp
from jax import lax
import numpy as np

D_MODEL = 1024
BATCH = 8
SEQ = 2048
DEPTH = 2

ATTN_HEADS = 8
ATTN_HEAD_DIM = 64
ATTN_V_DIM = 2 * ATTN_HEAD_DIM
ATTN_WIDTH = ATTN_HEADS * ATTN_V_DIM
Q_BLOCK = 128
NUM_BUCKETS = 32
MAX_DISTANCE = 128
SSM_INNER = 2 * D_MODEL
SSM_HEAD_DIM = 64
SSM_HEADS = SSM_INNER // SSM_HEAD_DIM
SSM_GROUPS = 4
SSM_STATE = 128
SSM_CONV = 5
SSM_CHUNK = 128
SSM_CONV_CH = SSM_INNER + 2 * SSM_GROUPS * SSM_STATE
MOE_GROUPS = 4
MOE_EXPERTS_PER_GROUP = 8
MOE_EXPERTS = MOE_GROUPS * MOE_EXPERTS_PER_GROUP
MOE_TOPK = 2
MOE_FF = 512
MOE_BLOCK = 128
Q_COLS = ATTN_HEADS * 2 * ATTN_HEAD_DIM
K_COLS = ATTN_HEADS * 2 * ATTN_HEAD_DIM
V_COLS = ATTN_WIDTH
Z_COLS = SSM_INNER
XBC_COLS = SSM_CONV_CH
DT_COLS = 2 * SSM_HEADS
GATE_COLS = D_MODEL
SPLITS = list(np.cumsum([Q_COLS, K_COLS, V_COLS, Z_COLS, XBC_COLS, DT_COLS, GATE_COLS]))
IN_COLS = Q_COLS + K_COLS + V_COLS + Z_COLS + XBC_COLS + DT_COLS + 2 * GATE_COLS
RMS_EPS = 1e-6

kernel_name = "hybrid_diffattn_ssd_hmoe_encoder"


def rms_norm(x, g):
    xf = x.astype(jnp.float32)
    y = xf * lax.rsqrt(jnp.mean(xf * xf, axis=-1, keepdims=True) + RMS_EPS)
    return (y * g.astype(jnp.float32)).astype(x.dtype)


def rel_bucket(rel):
    half = NUM_BUCKETS // 2
    max_exact = half // 2
    ret = jnp.where(rel > 0, half, 0)
    n = jnp.abs(rel)
    n_safe = jnp.maximum(n, 1).astype(jnp.float32)
    large = max_exact + (jnp.log(n_safe / max_exact) / math.log(MAX_DISTANCE / max_exact)
                         * (half - max_exact)).astype(jnp.int32)
    large = jnp.minimum(large, half - 1)
    return ret + jnp.where(n < max_exact, n, large)


def diff_attention(q, k, v, lam, rel_bias):
    b, s = q.shape[0], q.shape[1]
    nq = s // Q_BLOCK
    scale = ATTN_HEAD_DIM ** -0.5
    qb = q.reshape(b, nq, Q_BLOCK, ATTN_HEADS, 2, ATTN_HEAD_DIM).transpose(1, 0, 2, 3, 4, 5)
    k_pos = jnp.arange(s, dtype=jnp.int32)

    def one_block(args):
        q_blk, i = args
        q_pos = i * Q_BLOCK + jnp.arange(Q_BLOCK, dtype=jnp.int32)
        bucket = rel_bucket(k_pos[None, :] - q_pos[:, None])
        bias = rel_bias[bucket].astype(jnp.float32).transpose(2, 0, 1)
        logits = jnp.einsum("bqhmd,bkhmd->bhmqk", q_blk, k).astype(jnp.float32) * scale
        p = jax.nn.softmax(logits + bias[None, :, None], axis=-1)
        attn = p[:, :, 0] - lam * p[:, :, 1]
        return jnp.einsum("bhqk,bkhe->bqhe", attn.astype(v.dtype), v)

    out = lax.map(one_block, (qb, jnp.arange(nq, dtype=jnp.int32)))
    return out.transpose(1, 0, 2, 3, 4).reshape(b, s, ATTN_HEADS, ATTN_V_DIM)


def segsum(a):
    t = a.shape[-1]
    cum = jnp.cumsum(a, axis=-1)
    diff = cum[..., :, None] - cum[..., None, :]
    mask = jnp.tril(jnp.ones((t, t), dtype=bool))
    return jnp.where(mask, diff, -jnp.inf)


def ssd_scan(xdt, a, bm, cm):
    b, s, h, p = xdt.shape
    g, n = bm.shape[2], bm.shape[3]
    e = h // g
    c = s // SSM_CHUNK
    L = SSM_CHUNK
    X = xdt.reshape(b, c, L, g, e, p)
    a = a.astype(jnp.float32).reshape(b, c, L, g, e).transpose(0, 3, 4, 1, 2)
    Bc = bm.reshape(b, c, L, g, n)
    Cc = cm.reshape(b, c, L, g, n)
    a_cum = jnp.cumsum(a, axis=-1)
    lmat = jnp.exp(segsum(a))
    cb = jnp.einsum("bclgn,bcsgn->bcgls", Cc, Bc)
    y_diag = jnp.einsum("bcgls,bgecls,bcsgep->bclgep", cb, lmat, X)
    decay_states = jnp.exp(a_cum[..., -1:] - a_cum)
    states = jnp.einsum("bclgn,bgecl,bclgep->bcgepn", Bc, decay_states, X)
    states = jnp.concatenate([jnp.zeros_like(states[:, :1]), states], axis=1)
    chunk_a = jnp.pad(a_cum[..., -1], ((0, 0), (0, 0), (0, 0), (1, 0)))
    decay_chunk = jnp.exp(segsum(chunk_a))
    states_in = jnp.einsum("bgezc,bcgepn->bzgepn", decay_chunk, states)[:, :-1]
    y_off = jnp.einsum("bclgn,bcgepn,bgecl->bclgep", Cc, states_in, jnp.exp(a_cum))
    return (y_diag + y_off).reshape(b, s, h, p)


def ssd_mixer(z, xbc, dt_raw, conv_w, conv_b, dt_bias, a_log, d_skip, norm_g):
    b, s, _ = xbc.shape
    pad = SSM_CONV // 2
    u = lax.conv_general_dilated(xbc, conv_w.astype(xbc.dtype)[:, None, :], (1,), [(pad, pad)],
                                 dimension_numbers=("NWC", "WIO", "NWC"),
                                 feature_group_count=SSM_CONV_CH)
    u = jax.nn.silu(u + conv_b)
    xs, bm, cm = jnp.split(u, [SSM_INNER, SSM_INNER + SSM_GROUPS * SSM_STATE], axis=-1)
    X = xs.reshape(b, s, SSM_HEADS, SSM_HEAD_DIM)
    bm = bm.reshape(b, s, SSM_GROUPS, SSM_STATE)
    cm = cm.reshape(b, s, SSM_GROUPS, SSM_STATE)
    dt = jax.nn.softplus(dt_raw.astype(jnp.float32).reshape(b, s, 2, SSM_HEADS)
                         + dt_bias.astype(jnp.float32))
    A = -jnp.exp(a_log.astype(jnp.float32))
    y_f = ssd_scan(X * dt[:, :, 0, :, None], dt[:, :, 0] * A[0], bm, cm)
    flip = lambda t: jnp.flip(t, axis=1)
    y_b = flip(ssd_scan(flip(X * dt[:, :, 1, :, None]), flip(dt[:, :, 1] * A[1]), flip(bm), flip(cm)))
    y = y_f + y_b + X * d_skip[:, None]
    y = y.reshape(b, s, SSM_INNER).astype(z.dtype)
    return rms_norm(y * jax.nn.silu(z), norm_g)


def hier_moe(xn, w_rg, b_rg, w_re, b_re, w_gate, w_up, w_down):
    b, s, d = xn.shape
    t = b * s
    xf = xn.reshape(t, d)
    g_logits = (xf @ w_rg).astype(jnp.float32) + b_rg.astype(jnp.float32)
    g_prob = jax.nn.softmax(g_logits, axis=-1)
    g_idx = jnp.argmax(g_logits, axis=-1)
    g_gate = jnp.take_along_axis(g_prob, g_idx[:, None], axis=1)[:, 0]
    e_logits = ((xf @ w_re).astype(jnp.float32) + b_re.astype(jnp.float32)).reshape(
        t, MOE_GROUPS, MOE_EXPERTS_PER_GROUP)
    e_logits = jnp.take_along_axis(e_logits, g_idx[:, None, None], axis=1)[:, 0]
    top_vals, top_idx = lax.top_k(e_logits, MOE_TOPK)
    e_gate = jax.nn.softmax(top_vals, axis=-1) * g_gate[:, None]
    e_id = (g_idx[:, None] * MOE_EXPERTS_PER_GROUP + top_idx).reshape(-1).astype(jnp.int32)
    gate_flat = e_gate.reshape(-1)
    n_assign = t * MOE_TOPK
    order = jnp.argsort(e_id)
    e_sorted = e_id[order]
    tok_sorted = order // MOE_TOPK
    counts = jnp.zeros((MOE_EXPERTS,), jnp.int32).at[e_id].add(1)
    padded = (counts + MOE_BLOCK - 1) // MOE_BLOCK * MOE_BLOCK
    start = jnp.cumsum(counts) - counts
    pad_end = jnp.cumsum(padded)
    pad_start = pad_end - padded
    dest = pad_start[e_sorted] + jnp.arange(n_assign, dtype=jnp.int32) - start[e_sorted]
    cap = -(-n_assign // MOE_BLOCK) * MOE_BLOCK + MOE_EXPERTS * MOE_BLOCK
    n_blocks = cap // MOE_BLOCK
    x_pad = jnp.zeros((cap, d), xf.dtype).at[dest].set(xf[tok_sorted])
    blk_expert = jnp.minimum(
        jnp.searchsorted(pad_end, jnp.arange(n_blocks, dtype=jnp.int32) * MOE_BLOCK, side="right"),
        MOE_EXPERTS - 1)

    def expert_block(args):
        xb, e = args
        hdn = jax.nn.silu(xb @ w_gate[e]) * (xb @ w_up[e])
        return hdn @ w_down[e]

    y_pad = lax.map(expert_block, (x_pad.reshape(n_blocks, MOE_BLOCK, d), blk_expert)).reshape(cap, d)
    y_sorted = y_pad[dest] * gate_flat[order][:, None].astype(y_pad.dtype)
    out = jax.ops.segment_sum(y_sorted, tok_sorted, num_segments=t)
    return out.reshape(b, s, d).astype(xn.dtype)


def setup_inputs(seed: int = 0) -> dict:
    key = jax.random.key(seed)
    ks = jax.random.split(key, 26)
    f32 = jnp.float32
    L = DEPTH

    def nrm(k, shape, scale):
        return jax.random.normal(k, shape, f32) * scale

    dt0 = jnp.exp(jax.random.uniform(ks[9], (L, 2, SSM_HEADS), f32, math.log(1e-3), math.log(1e-1)))
    return {
        "x": nrm(ks[0], (BATCH, SEQ, D_MODEL), 1.0),
        "norm1_g": 1.0 + nrm(ks[1], (L, D_MODEL), 0.02),
        "w_in": nrm(ks[2], (L, D_MODEL, IN_COLS), D_MODEL ** -0.5),
        "qk_norm_g": 1.0 + nrm(ks[3], (L, 2, ATTN_HEAD_DIM), 0.02),
        "lambda_qk": nrm(ks[4], (L, 4, ATTN_HEAD_DIM), 0.1),
        "attn_head_norm_g": 1.0 + nrm(ks[5], (L, ATTN_V_DIM), 0.02),
        "rel_bias": nrm(ks[6], (NUM_BUCKETS, ATTN_HEADS), 0.5),
        "conv_w": nrm(ks[7], (L, SSM_CONV, SSM_CONV_CH), SSM_CONV ** -0.5),
        "conv_b": nrm(ks[8], (L, SSM_CONV_CH), 0.02),
        "dt_bias": dt0 + jnp.log(-jnp.expm1(-dt0)),
        "a_log": jnp.log(jax.random.uniform(ks[10], (L, 2, SSM_HEADS), f32, 1.0, 16.0)),
        "d_skip": 1.0 + nrm(ks[11], (L, SSM_HEADS), 0.1),
        "ssm_norm_g": 1.0 + nrm(ks[12], (L, SSM_INNER), 0.02),
        "w_attn_out": nrm(ks[13], (L, ATTN_WIDTH, D_MODEL), ATTN_WIDTH ** -0.5),
        "w_ssm_out": nrm(ks[14], (L, SSM_INNER, D_MODEL), SSM_INNER ** -0.5),
        "w_out": nrm(ks[15], (L, D_MODEL, D_MODEL), D_MODEL ** -0.5),
        "norm2_g": 1.0 + nrm(ks[16], (L, D_MODEL), 0.02),
        "w_router_group": nrm(ks[17], (L, D_MODEL, MOE_GROUPS), D_MODEL ** -0.5),
        "b_router_group": nrm(ks[18], (L, MOE_GROUPS), 0.01),
        "w_router_expert": nrm(ks[19], (L, D_MODEL, MOE_EXPERTS), D_MODEL ** -0.5),
        "b_router_expert": nrm(ks[20], (L, MOE_EXPERTS), 0.01),
        "w_exp_gate": nrm(ks[21], (L, MOE_EXPERTS, D_MODEL, MOE_FF), D_MODEL ** -0.5),
        "w_exp_up": nrm(ks[22], (L, MOE_EXPERTS, D_MODEL, MOE_FF), D_MODEL ** -0.5),
        "w_exp_down": nrm(ks[23], (L, MOE_EXPERTS, MOE_FF, D_MODEL), MOE_FF ** -0.5),
    }


def reference(x, norm1_g, w_in, qk_norm_g, lambda_qk, attn_head_norm_g, rel_bias, conv_w, conv_b,
              dt_bias, a_log, d_skip, ssm_norm_g, w_attn_out, w_ssm_out, w_out, norm2_g,
              w_router_group, b_router_group, w_router_expert, b_router_expert,
              w_exp_gate, w_exp_up, w_exp_down):
    b, s, _ = x.shape
    for i in range(DEPTH):
        h = rms_norm(x, norm1_g[i])
        proj = h @ w_in[i]
        q, k, v, z, xbc, dt_raw, gate_a, gate_m = jnp.split(proj, SPLITS, axis=-1)

        q = rms_norm(q.reshape(b, s, ATTN_HEADS, 2, ATTN_HEAD_DIM), qk_norm_g[i, 0])
        k = rms_norm(k.reshape(b, s, ATTN_HEADS, 2, ATTN_HEAD_DIM), qk_norm_g[i, 1])
        v = v.reshape(b, s, ATTN_HEADS, ATTN_V_DIM)
        lam_init = 0.8 - 0.6 * math.exp(-0.3 * i)
        lq = lambda_qk[i].astype(jnp.float32)
        lam = (jnp.exp(jnp.sum(lq[0] * lq[1])) - jnp.exp(jnp.sum(lq[2] * lq[3])) + lam_init)
        ya = diff_attention(q, k, v, lam, rel_bias)
        ya = (rms_norm(ya, attn_head_norm_g[i]) * (1.0 - lam_init)).reshape(b, s, ATTN_WIDTH)
        ya = ya @ w_attn_out[i]

        ym = ssd_mixer(z, xbc, dt_raw, conv_w[i], conv_b[i], dt_bias[i], a_log[i], d_skip[i],
                       ssm_norm_g[i])
        ym = ym @ w_ssm_out[i]

        merged = jax.nn.sigmoid(gate_a) * ya + jax.nn.sigmoid(gate_m) * ym
        x = x + (merged @ w_out[i]).astype(x.dtype)

        hn = rms_norm(x, norm2_g[i])
        x = x + hier_moe(hn, w_router_group[i], b_router_group[i], w_router_expert[i],
                         b_router_expert[i], w_exp_gate[i], w_exp_up[i], w_exp_down[i]).astype(x.dtype)
    return x
```

```python
import functools
import math

import jax
import jax.numpy as jnp
import numpy as np
from jax import lax
from jax.experimental import pallas as pl
from jax.experimental.pallas import tpu as pltpu

D_MODEL = 1024
DEPTH = 2
ATTN_HEADS = 8
ATTN_HEAD_DIM = 64
ATTN_V_DIM = 2 * ATTN_HEAD_DIM
ATTN_WIDTH = ATTN_HEADS * ATTN_V_DIM
NUM_BUCKETS = 32
MAX_DISTANCE = 128
SSM_INNER = 2 * D_MODEL
SSM_HEAD_DIM = 64
SSM_HEADS = SSM_INNER // SSM_HEAD_DIM
SSM_GROUPS = 4
SSM_HEADS_PER_GROUP = SSM_HEADS // SSM_GROUPS
SSM_STATE = 128
SSM_CONV = 5
SSM_CHUNK = 128
SSM_CONV_CH = SSM_INNER + 2 * SSM_GROUPS * SSM_STATE
MOE_GROUPS = 4
MOE_EXPERTS_PER_GROUP = 8
MOE_EXPERTS = MOE_GROUPS * MOE_EXPERTS_PER_GROUP
MOE_TOPK = 2
MOE_FF = 512
MOE_BLOCK = 128
RMS_EPS = 1e-6

LANES = 128
COL_Z = 0
COL_Q = COL_Z + SSM_INNER
COL_K = COL_Q + ATTN_WIDTH
COL_V = COL_K + ATTN_WIDTH
COL_XBC = COL_V + ATTN_WIDTH
COL_GA = COL_XBC + SSM_CONV_CH
COL_GM = COL_GA + D_MODEL
MAIN_COLS = COL_GM + D_MODEL
DT_COLS = 2 * SSM_HEADS
DT_LANES_PER_GROUP = 2 * SSM_HEADS_PER_GROUP

VMEM_LIMIT_BYTES = 56 * 1024 * 1024

F32 = jnp.float32
BF16 = jnp.bfloat16


def _params(*semantics):
    return pltpu.CompilerParams(dimension_semantics=semantics, vmem_limit_bytes=VMEM_LIMIT_BYTES)


def _rms(x, g):
    ms = jnp.mean(x * x, axis=-1, keepdims=True)
    return x * lax.rsqrt(ms + RMS_EPS) * g


def _norm_matmul_kernel(x_ref, g_ref, w_ref, o_ref, hn_ref):
    @pl.when(pl.program_id(1) == 0)
    def _():
        hn_ref[...] = _rms(x_ref[...], g_ref[...]).astype(hn_ref.dtype)

    o_ref[...] = jnp.dot(hn_ref[...], w_ref[...], preferred_element_type=F32).astype(o_ref.dtype)


def _norm_matmul(x, g, w, out_dtype, tm, tn):
    t, d = x.shape
    n = w.shape[1]
    return pl.pallas_call(
        _norm_matmul_kernel,
        grid=(t // tm, n // tn),
        in_specs=[
            pl.BlockSpec((tm, d), lambda i, j: (i, 0)),
            pl.BlockSpec((1, d), lambda i, j: (0, 0)),
            pl.BlockSpec((d, tn), lambda i, j: (0, j)),
        ],
        out_specs=pl.BlockSpec((tm, tn), lambda i, j: (i, j)),
        out_shape=jax.ShapeDtypeStruct((t, n), out_dtype),
        scratch_shapes=[pltpu.VMEM((tm, d), BF16)],
        compiler_params=_params("parallel", "arbitrary"),
        name="in_proj",
    )(x, g, w)


def _dt_prep_kernel(x_ref, g_ref, w_ref, bias_ref, a_ref, dt_ref, ac_ref, act_ref):
    hn = _rms(x_ref[...], g_ref[...]).astype(BF16)
    raw = jnp.dot(hn, w_ref[...], preferred_element_type=F32) + bias_ref[...]
    dt = jnp.maximum(raw, 0.0) + jnp.log1p(jnp.exp(-jnp.abs(raw)))
    a = dt * a_ref[...]
    s = a.shape[0]
    pos = lax.broadcasted_iota(jnp.int32, a.shape, 0) & (SSM_CHUNK - 1)
    pre = a
    suf = a
    k = 1
    while k < SSM_CHUNK:
        pre = pre + jnp.where(pos >= k, pltpu.roll(pre, k, 0), 0.0)
        suf = suf + jnp.where(pos < SSM_CHUNK - k, pltpu.roll(suf, s - k, 0), 0.0)
        k *= 2
    lane = lax.broadcasted_iota(jnp.int32, a.shape, 1)
    is_fwd = (lane & (DT_LANES_PER_GROUP - 1)) < SSM_HEADS_PER_GROUP
    ac = jnp.where(is_fwd, pre, suf)
    dt_ref[...] = dt
    ac_ref[...] = ac
    act_ref[...] = ac.T


def _dt_prep(x3, g, w_dt, bias, a_neg):
    b, s, d = x3.shape
    out = jax.ShapeDtypeStruct((b, s, LANES), F32)
    out_t = jax.ShapeDtypeStruct((b, LANES, s), F32)
    vec = pl.BlockSpec((1, LANES), lambda i: (0, 0))
    return pl.pallas_call(
        _dt_prep_kernel,
        grid=(b,),
        in_specs=[
            pl.BlockSpec((None, s, d), lambda i: (i, 0, 0)),
            pl.BlockSpec((1, d), lambda i: (0, 0)),
            pl.BlockSpec((d, LANES), lambda i: (0, 0)),
            vec,
            vec,
        ],
        out_specs=[
            pl.BlockSpec((None, s, LANES), lambda i: (i, 0, 0)),
            pl.BlockSpec((None, s, LANES), lambda i: (i, 0, 0)),
            pl.BlockSpec((None, LANES, s), lambda i: (i, 0, 0)),
        ],
        out_shape=[out, out, out_t],
        compiler_params=_params("parallel"),
        name="dt_prep",
    )(x3, g, w_dt, bias, a_neg)


def _conv_kernel(x_ref, w_ref, b_ref, o_ref):
    xf = x_ref[...].astype(F32)
    s = xf.shape[0]
    row = lax.broadcasted_iota(jnp.int32, xf.shape, 0)
    pad = SSM_CONV // 2
    acc = xf * w_ref[pad:pad + 1, :] + b_ref[...]
    for j in range(SSM_CONV):
        shift = pad - j
        if shift == 0:
            continue
        rolled = pltpu.roll(xf, shift % s, 0)
        valid = (row >= shift) & (row < s + shift)
        acc = acc + jnp.where(valid, rolled, 0.0) * w_ref[j:j + 1, :]
    o_ref[...] = (acc * jax.nn.sigmoid(acc)).astype(o_ref.dtype)


def _conv_silu(proj3, conv_w, conv_b, tc):
    b, s, _ = proj3.shape
    col0 = COL_XBC // tc
    return pl.pallas_call(
        _conv_kernel,
        grid=(b, SSM_CONV_CH // tc),
        in_specs=[
            pl.BlockSpec((None, s, tc), lambda i, c: (i, 0, col0 + c)),
            pl.BlockSpec((SSM_CONV, tc), lambda i, c: (0, c)),
            pl.BlockSpec((1, tc), lambda i, c: (0, c)),
        ],
        out_specs=pl.BlockSpec((None, s, tc), lambda i, c: (i, 0, c)),
        out_shape=jax.ShapeDtypeStruct((b, s, SSM_CONV_CH), BF16),
        compiler_params=_params("parallel", "parallel"),
        name="conv_silu",
    )(proj3, conv_w, conv_b)


def _rel_bucket(rel):
    half = NUM_BUCKETS // 2
    max_exact = half // 2
    ret = jnp.where(rel > 0, half, 0)
    n = jnp.abs(rel)
    n_safe = jnp.maximum(n, 1).astype(F32)
    large = max_exact + (jnp.log(n_safe / max_exact) / math.log(MAX_DISTANCE / max_exact)
                         * (half - max_exact)).astype(jnp.int32)
    large = jnp.minimum(large, half - 1)
    return ret + jnp.where(n < max_exact, n, large)


def _bias_kernel(relb_ref, bid_ref, o_ref, *, tq):
    h = pl.program_id(0)
    bid = bid_ref[...]
    fv = jnp.zeros(bid.shape, F32)
    for bkt in range(NUM_BUCKETS):
        fv = jnp.where(bid == bkt, relb_ref[h, bkt], fv)
    g = jnp.broadcast_to(fv, o_ref.shape)
    row = lax.broadcasted_iota(jnp.int32, o_ref.shape, 0)
    k = 1
    while k < tq:
        g = jnp.where((row & k) != 0, pltpu.roll(g, k, 1), g)
        k *= 2
    o_ref[...] = g


def _bias_windows(rel_bias, s, tq):
    width = 2 * s
    m = jnp.arange(width, dtype=jnp.int32)
    bid = _rel_bucket(((m + tq) % width) - s).astype(jnp.int32)[None, :]
    return pl.pallas_call(
        functools.partial(_bias_kernel, tq=tq),
        grid=(ATTN_HEADS,),
        in_specs=[
            pl.BlockSpec(memory_space=pltpu.SMEM),
            pl.BlockSpec((1, width), lambda h: (0, 0)),
        ],
        out_specs=pl.BlockSpec((None, tq, width), lambda h: (h, 0, 0)),
        out_shape=jax.ShapeDtypeStruct((ATTN_HEADS, tq, width), F32),
        compiler_params=_params("parallel"),
        name="rel_bias_windows",
    )(rel_bias.T.astype(F32), bid)


def _attn_kernel(q_ref, k_ref, v_ref, bias_ref, gqk_ref, lamqk_ref, gh_ref, o_ref, kn_ref, *, tq, lam_init):
    qi = pl.program_id(2)
    s = k_ref.shape[0]
    lo = lax.broadcasted_iota(jnp.int32, (1, LANES), 1) < ATTN_HEAD_DIM

    def half_norm(x, g):
        x2 = x * x
        s_all = jnp.sum(x2, axis=-1, keepdims=True)
        s_lo = jnp.sum(jnp.where(lo, x2, 0.0), axis=-1, keepdims=True)
        ms = jnp.where(lo, s_lo, s_all - s_lo) * (1.0 / ATTN_HEAD_DIM)
        return x * lax.rsqrt(ms + RMS_EPS) * g

    @pl.when(qi == 0)
    def _():
        kn_ref[...] = half_norm(k_ref[...].astype(F32), gqk_ref[1:2, :]).astype(kn_ref.dtype)

    qn = half_norm(q_ref[...].astype(F32), gqk_ref[0:1, :]) * (ATTN_HEAD_DIM ** -0.5)
    c0 = pl.multiple_of(s - (qi + 1) * tq, tq)
    bias = bias_ref[:, pl.ds(c0, s)]
    kn = kn_ref[...]

    def softmax_map(qm):
        logits = lax.dot_general(qm.astype(BF16), kn, (((1,), (1,)), ((), ())), preferred_element_type=F32) + bias
        p = jnp.exp(logits - jnp.max(logits, axis=-1, keepdims=True))
        return p, jnp.sum(p, axis=-1, keepdims=True)

    p0, l0 = softmax_map(jnp.where(lo, qn, 0.0))
    p1, l1 = softmax_map(jnp.where(lo, 0.0, qn))
    lq = lamqk_ref[...]
    lam = (jnp.exp(jnp.sum(lq[0:1] * lq[1:2], axis=-1, keepdims=True))
           - jnp.exp(jnp.sum(lq[2:3] * lq[3:4], axis=-1, keepdims=True)) + lam_init)
    attn = p0 * (1.0 / l0) - p1 * (lam / l1)
    out = jnp.dot(attn.astype(BF16), v_ref[...], preferred_element_type=F32)
    o_ref[...] = (_rms(out, gh_ref[...]) * (1.0 - lam_init)).astype(o_ref.dtype)


def _diff_attention(proj3, bias_win, gqk, lamqk, gh, tq, lam_init):
    b, s, _ = proj3.shape
    width = bias_win.shape[-1]
    return pl.pallas_call(
        functools.partial(_attn_kernel, tq=tq, lam_init=lam_init),
        grid=(ATTN_HEADS, b, s // tq),
        in_specs=[
            pl.BlockSpec((None, tq, LANES), lambda h, i, q: (i, q, COL_Q // LANES + h)),
            pl.BlockSpec((None, s, LANES), lambda h, i, q: (i, 0, COL_K // LANES + h)),
            pl.BlockSpec((None, s, LANES), lambda h, i, q: (i, 0, COL_V // LANES + h)),
            pl.BlockSpec((None, tq, width), lambda h, i, q: (h, 0, 0)),
            pl.BlockSpec((2, LANES), lambda h, i, q: (0, 0)),
            pl.BlockSpec((4, ATTN_HEAD_DIM), lambda h, i, q: (0, 0)),
            pl.BlockSpec((1, LANES), lambda h, i, q: (0, 0)),
        ],
        out_specs=pl.BlockSpec((None, tq, LANES), lambda h, i, q: (i, q, h)),
        out_shape=jax.ShapeDtypeStruct((b, s, ATTN_WIDTH), BF16),
        scratch_shapes=[pltpu.VMEM((s, LANES), BF16)],
        compiler_params=_params("parallel", "parallel", "arbitrary"),
        name="diff_attention",
    )(proj3, proj3, proj3, bias_win, gqk, lamqk, gh)


def _ssd_kernel(x_ref, b_ref, c_ref, dt_ref, ac_ref, act_ref, dsk_ref, y_ref, st_ref, acc_ref):
    L = SSM_CHUNK
    s = x_ref.shape[0]
    n_chunks = s // L
    row = lax.broadcasted_iota(jnp.int32, (L, L), 0)
    col = lax.broadcasted_iota(jnp.int32, (L, L), 1)
    lo = lax.broadcasted_iota(jnp.int32, (L, LANES), 1) < SSM_HEAD_DIM
    lo1 = lo[0:1, :]
    n_pairs = SSM_HEADS_PER_GROUP // 2

    def one_chunk(c, backward):
        r0 = pl.multiple_of(c * L, L)
        off = SSM_HEADS_PER_GROUP if backward else 0
        tot_row = 0 if backward else L - 1
        mask = (row <= col) if backward else (row >= col)
        xc = x_ref[pl.ds(r0, L), :].astype(F32)
        bc = b_ref[pl.ds(r0, L), :]
        cc = c_ref[pl.ds(r0, L), :]
        dtc = dt_ref[pl.ds(r0, L), :]
        acc = ac_ref[pl.ds(r0, L), :]
        act = act_ref[:, pl.ds(r0, L)]
        cb = lax.dot_general(cc, bc, (((1,), (1,)), ((), ())), preferred_element_type=F32)
        y_off = jnp.dot(cc, st_ref[...].astype(BF16), preferred_element_type=F32)

        def col_of(arr, e):
            return jnp.broadcast_to(arr[:, off + e:off + e + 1], (L, LANES))

        ys, xdecs, tots = [], [], []
        for j in range(n_pairs):
            e0, e1 = 2 * j, 2 * j + 1
            dt_pair = jnp.where(lo, col_of(dtc, e0), col_of(dtc, e1))
            ac_pair = jnp.where(lo, col_of(acc, e0), col_of(acc, e1))
            tot_pair = jnp.where(lo1, acc[tot_row:tot_row + 1, off + e0:off + e0 + 1],
                                 acc[tot_row:tot_row + 1, off + e1:off + e1 + 1])
            xdt = xc[:, j * LANES:(j + 1) * LANES] * dt_pair
            xdt_b = xdt.astype(BF16)
            diag = []
            for e in (e0, e1):
                decay = jnp.where(mask, jnp.exp(col_of(acc, e) - act[off + e:off + e + 1, :]), 0.0)
                diag.append(jnp.dot((cb * decay).astype(BF16), xdt_b, preferred_element_type=F32))
            ys.append(jnp.where(lo, diag[0], diag[1]) + y_off[:, j * LANES:(j + 1) * LANES] * jnp.exp(ac_pair))
            xdecs.append((xdt * jnp.exp(tot_pair - ac_pair)).astype(BF16))
            tots.append(jnp.exp(tot_pair))
        y = jnp.concatenate(ys, axis=1)
        xdec = jnp.concatenate(xdecs, axis=1)
        st_new = lax.dot_general(bc, xdec, (((0,), (0,)), ((), ())), preferred_element_type=F32)
        st_ref[...] = st_ref[...] * jnp.concatenate(tots, axis=1) + st_new
        if backward:
            y_ref[pl.ds(r0, L), :] = (acc_ref[pl.ds(r0, L), :] + y).astype(y_ref.dtype)
        else:
            acc_ref[pl.ds(r0, L), :] = y + xc * dsk_ref[...]

    st_ref[...] = jnp.zeros_like(st_ref)

    def fwd_body(c, carry):
        one_chunk(c, False)
        return carry

    lax.fori_loop(0, n_chunks, fwd_body, 0)
    st_ref[...] = jnp.zeros_like(st_ref)

    def bwd_body(c, carry):
        one_chunk(n_chunks - 1 - c, True)
        return carry

    lax.fori_loop(0, n_chunks, bwd_body, 0)


def _ssd_scan(u, dt_g, ac_g, act_g, dskip):
    b, s, _ = u.shape
    gw = SSM_HEADS_PER_GROUP * SSM_HEAD_DIM
    xcols = SSM_INNER // SSM_STATE
    nl = DT_LANES_PER_GROUP
    return pl.pallas_call(
        _ssd_kernel,
        grid=(b, SSM_GROUPS),
        in_specs=[
            pl.BlockSpec((None, s, gw), lambda i, g: (i, 0, g)),
            pl.BlockSpec((None, s, SSM_STATE), lambda i, g: (i, 0, xcols + g)),
            pl.BlockSpec((None, s, SSM_STATE), lambda i, g: (i, 0, xcols + SSM_GROUPS + g)),
            pl.BlockSpec((None, None, s, nl), lambda i, g: (i, g, 0, 0)),
            pl.BlockSpec((None, None, s, nl), lambda i, g: (i, g, 0, 0)),
            pl.BlockSpec((None, None, nl, s), lambda i, g: (i, g, 0, 0)),
            pl.BlockSpec((1, gw), lambda i, g: (0, g)),
        ],
        out_specs=pl.BlockSpec((None, s, gw), lambda i, g: (i, 0, g)),
        out_shape=jax.ShapeDtypeStruct((b, s, SSM_INNER), BF16),
        scratch_shapes=[pltpu.VMEM((SSM_STATE, gw), F32), pltpu.VMEM((s, gw), F32)],
        compiler_params=_params("parallel", "parallel"),
        name="ssd_scan",
    )(u, u, u, dt_g, ac_g, act_g, dskip)


def _route(logits):
    lane = lax.broadcasted_iota(jnp.int32, logits.shape, 1)
    lane_f = lane.astype(F32)
    neg = -jnp.inf
    big = float(LANES)
    is_g = lane < MOE_GROUPS
    gl = jnp.where(is_g, logits, neg)
    gmax = jnp.max(gl, axis=-1, keepdims=True)
    g_idx = jnp.min(jnp.where(gl == gmax, lane_f, big), axis=-1, keepdims=True)
    g_gate = 1.0 / jnp.sum(jnp.where(is_g, jnp.exp(logits - gmax), 0.0), axis=-1, keepdims=True)
    first = MOE_GROUPS + MOE_EXPERTS_PER_GROUP * g_idx
    in_grp = (lane_f >= first) & (lane_f < first + MOE_EXPERTS_PER_GROUP)
    el = jnp.where(in_grp, logits, neg)
    v1 = jnp.max(el, axis=-1, keepdims=True)
    i1 = jnp.min(jnp.where(el == v1, lane_f, big), axis=-1, keepdims=True)
    el2 = jnp.where(lane_f == i1, neg, el)
    v2 = jnp.max(el2, axis=-1, keepdims=True)
    i2 = jnp.min(jnp.where(el2 == v2, lane_f, big), axis=-1, keepdims=True)
    r = jnp.exp(v2 - v1)
    p1 = 1.0 / (1.0 + r)
    p2 = r * p1
    ids = jnp.where(lane == 0, i1 - MOE_GROUPS, jnp.where(lane == 1, i2 - MOE_GROUPS, 0.0)).astype(jnp.int32)
    gates = jnp.where(lane == 0, p1 * g_gate, jnp.where(lane == 1, p2 * g_gate, 0.0))
    return ids, gates


def _post_kernel(attn_ref, y_ref, z_ref, ga_ref, gm_ref, x_ref, gs_ref, wa_ref, ws_ref, wo_ref, g2_ref, wr_ref,
                 br_ref, xo_ref, hn_ref, id_ref, gate_ref):
    ya = jnp.dot(attn_ref[...], wa_ref[...], preferred_element_type=F32)
    z = z_ref[...].astype(F32)
    gated = y_ref[...].astype(F32) * (z * jax.nn.sigmoid(z))
    ym = jnp.dot(_rms(gated, gs_ref[...]).astype(BF16), ws_ref[...], preferred_element_type=F32)
    merged = (jax.nn.sigmoid(ga_ref[...].astype(F32)) * ya + jax.nn.sigmoid(gm_ref[...].astype(F32)) * ym)
    x_new = x_ref[...] + jnp.dot(merged.astype(BF16), wo_ref[...], preferred_element_type=F32)
    xo_ref[...] = x_new
    hn = _rms(x_new, g2_ref[...])
    hn_ref[...] = hn
    logits = jnp.dot(hn, wr_ref[...], preferred_element_type=F32, precision=lax.Precision.HIGHEST) + br_ref[...]
    ids, gates = _route(logits)
    id_ref[...] = ids
    gate_ref[...] = gates


def _post_mixer(attn, y, proj, x, gs, wa, ws, wo, g2, wr, br, tm):
    t, d = x.shape
    row = lambda w, c: pl.BlockSpec((tm, w), lambda i: (i, c))
    full = lambda a: pl.BlockSpec(a.shape, lambda i: (0,) * a.ndim)
    return pl.pallas_call(
        _post_kernel,
        grid=(t // tm,),
        in_specs=[
            row(ATTN_WIDTH, 0),
            row(SSM_INNER, 0),
            row(SSM_INNER, COL_Z // SSM_INNER),
            row(D_MODEL, COL_GA // D_MODEL),
            row(D_MODEL, COL_GM // D_MODEL),
            row(D_MODEL, 0),
            full(gs), full(wa), full(ws), full(wo), full(g2), full(wr), full(br),
        ],
        out_specs=[row(D_MODEL, 0), row(D_MODEL, 0), row(LANES, 0), row(LANES, 0)],
        out_shape=[
            jax.ShapeDtypeStruct((t, d), F32),
            jax.ShapeDtypeStruct((t, d), F32),
            jax.ShapeDtypeStruct((t, LANES), jnp.int32),
            jax.ShapeDtypeStruct((t, LANES), F32),
        ],
        compiler_params=_params("parallel"),
        name="post_mixer",
    )(attn, y, proj, proj, proj, x, gs, wa, ws, wo, g2, wr, br)


def _row_copies(idx_ref, base, src_hbm, dst_ref, sem, start):
    def body(r, carry):
        cp = pltpu.make_async_copy(src_hbm.at[pl.ds(idx_ref[base + r], 1), :], dst_ref.at[pl.ds(r, 1), :], sem)
        if start:
            cp.start()
        else:
            cp.wait()
        return carry

    lax.fori_loop(0, dst_ref.shape[0], body, 0)


def _moe_kernel(blk_e_ref, src_ref, nused_ref, hn_hbm, wg_ref, wu_ref, wd_ref, y_ref, xg_ref, sem):
    i = pl.program_id(0)
    nused = nused_ref[0]

    def gather(blk, start):
        slot = blk % 2
        _row_copies(src_ref, blk * MOE_BLOCK, hn_hbm, xg_ref.at[slot], sem.at[slot], start)

    @pl.when(i == 0)
    def _():
        gather(0, True)

    @pl.when(i + 1 < nused)
    def _():
        gather(i + 1, True)

    @pl.when(i < nused)
    def _():
        gather(i, False)
        xb = xg_ref[i % 2].astype(BF16)
        hg = jnp.dot(xb, wg_ref[...], preferred_element_type=F32)
        hu = jnp.dot(xb, wu_ref[...], preferred_element_type=F32)
        hid = (hg * jax.nn.sigmoid(hg) * hu).astype(BF16)
        y_ref[...] = jnp.dot(hid, wd_ref[...], preferred_element_type=F32)

    @pl.when(i >= nused)
    def _():
        y_ref[...] = jnp.zeros_like(y_ref)


def _moe_experts(blk_expert, src_tok, nused, hn, wg, wu, wd):
    n_blocks = blk_expert.shape[0]
    d = hn.shape[1]
    grid_spec = pltpu.PrefetchScalarGridSpec(
        num_scalar_prefetch=3,
        grid=(n_blocks,),
        in_specs=[
            pl.BlockSpec(memory_space=pl.ANY),
            pl.BlockSpec((None, d, MOE_FF), lambda i, be, st, nu: (be[i], 0, 0)),
            pl.BlockSpec((None, d, MOE_FF), lambda i, be, st, nu: (be[i], 0, 0)),
            pl.BlockSpec((None, MOE_FF, d), lambda i, be, st, nu: (be[i], 0, 0)),
        ],
        out_specs=pl.BlockSpec((MOE_BLOCK, d), lambda i, be, st, nu: (i, 0)),
        scratch_shapes=[pltpu.VMEM((2, MOE_BLOCK, d), F32), pltpu.SemaphoreType.DMA((2,))],
    )
    return pl.pallas_call(
        _moe_kernel,
        grid_spec=grid_spec,
        out_shape=jax.ShapeDtypeStruct((n_blocks * MOE_BLOCK, d), F32),
        compiler_params=_params("arbitrary"),
        name="moe_experts",
    )(blk_expert, src_tok, nused, hn, wg, wu, wd)


def _combine_kernel(pos_ref, x_ref, gate_ref, y_hbm, o_ref, yg_ref, sem, *, tt):
    i = pl.program_id(0)
    n = pl.num_programs(0)

    def gather(tile, start):
        slot = tile % 2
        for k in range(MOE_TOPK):
            _row_copies(pos_ref, (k * n + tile) * tt, y_hbm, yg_ref.at[slot, k], sem.at[slot], start)

    @pl.when(i == 0)
    def _():
        gather(0, True)

    @pl.when(i + 1 < n)
    def _():
        gather(i + 1, True)

    gather(i, False)
    slot = i % 2
    gates = gate_ref[...]
    o_ref[...] = x_ref[...] + gates[:, 0:1] * yg_ref[slot, 0] + gates[:, 1:2] * yg_ref[slot, 1]


def _moe_combine(pos_kt, x, gates, y_pad, tt):
    t, d = x.shape
    grid_spec = pltpu.PrefetchScalarGridSpec(
        num_scalar_prefetch=1,
        grid=(t // tt,),
        in_specs=[
            pl.BlockSpec((tt, d), lambda i, p: (i, 0)),
            pl.BlockSpec((tt, LANES), lambda i, p: (i, 0)),
            pl.BlockSpec(memory_space=pl.ANY),
        ],
        out_specs=pl.BlockSpec((tt, d), lambda i, p: (i, 0)),
        scratch_shapes=[pltpu.VMEM((2, MOE_TOPK, tt, d), F32), pltpu.SemaphoreType.DMA((2,))],
    )
    return pl.pallas_call(
        functools.partial(_combine_kernel, tt=tt),
        grid_spec=grid_spec,
        out_shape=jax.ShapeDtypeStruct((t, d), F32),
        compiler_params=_params("arbitrary"),
        name="moe_combine",
    )(pos_kt, x, gates, y_pad)


def _dispatch_plan(ids):
    t = ids.shape[0]
    n_assign = t * MOE_TOPK
    e_flat = ids.reshape(-1)
    onehot = (e_flat[:, None] == jnp.arange(MOE_EXPERTS, dtype=jnp.int32)[None, :]).astype(jnp.int32)
    csum = jnp.cumsum(onehot, axis=0)
    rank = jnp.sum(csum * onehot, axis=1) - 1
    counts = csum[-1]
    padded = (counts + MOE_BLOCK - 1) // MOE_BLOCK * MOE_BLOCK
    pad_end = jnp.cumsum(padded)
    pad_start = pad_end - padded
    dest = jnp.sum(pad_start[None, :] * onehot, axis=1) + rank
    n_blocks = -(-n_assign // MOE_BLOCK) + MOE_EXPERTS
    blk_expert = jnp.minimum(
        jnp.searchsorted(pad_end, jnp.arange(n_blocks, dtype=jnp.int32) * MOE_BLOCK, side="right"),
        MOE_EXPERTS - 1).astype(jnp.int32)
    tok = jnp.arange(n_assign, dtype=jnp.int32) // MOE_TOPK
    src_tok = jnp.zeros((n_blocks * MOE_BLOCK,), jnp.int32).at[dest].set(tok)
    nused = (pad_end[-1:] // MOE_BLOCK).astype(jnp.int32)
    pos_kt = dest.reshape(t, MOE_TOPK).T.reshape(-1).astype(jnp.int32)
    return blk_expert, src_tok, nused, pos_kt


def _pick(n, pref):
    return pref if n % pref == 0 else n


def kernel(x, norm1_g, w_in, qk_norm_g, lambda_qk, attn_head_norm_g, rel_bias, conv_w, conv_b, dt_bias, a_log,
           d_skip, ssm_norm_g, w_attn_out, w_ssm_out, w_out, norm2_g, w_router_group, b_router_group,
           w_router_expert, b_router_expert, w_exp_gate, w_exp_up, w_exp_down):
    b, s, d = x.shape
    t = b * s
    tq = _pick(s, 256)
    bias_win = _bias_windows(rel_bias, s, tq)

    z_col0 = 3 * ATTN_WIDTH
    xbc_col0 = z_col0 + SSM_INNER
    dt_col0 = xbc_col0 + SSM_CONV_CH
    lane = np.arange(DT_COLS)
    grp, rem = lane // DT_LANES_PER_GROUP, lane % DT_LANES_PER_GROUP
    direction, e = rem // SSM_HEADS_PER_GROUP, rem % SSM_HEADS_PER_GROUP
    dt_src = direction * SSM_HEADS + grp * SSM_HEADS_PER_GROUP + e
    pad_lanes = LANES - DT_COLS

    xf = x.reshape(t, d)
    for i in range(DEPTH):
        lam_init = 0.8 - 0.6 * math.exp(-0.3 * i)
        g1 = norm1_g[i][None, :]
        w_main = jnp.concatenate([w_in[i][:, z_col0:xbc_col0], w_in[i][:, :z_col0], w_in[i][:, xbc_col0:dt_col0],
                                  w_in[i][:, dt_col0 + DT_COLS:]], axis=1).astype(BF16)
        w_dt = jnp.pad(w_in[i][:, dt_col0 + dt_src], ((0, 0), (0, pad_lanes))).astype(BF16)
        dt_b = jnp.pad(dt_bias[i].reshape(-1)[dt_src], (0, pad_lanes))[None, :]
        a_neg = jnp.pad(-jnp.exp(a_log[i].astype(F32)).reshape(-1)[dt_src], (0, pad_lanes))[None, :]

        proj = _norm_matmul(xf, g1, w_main, BF16, _pick(t, 1024), 1024)
        proj3 = proj.reshape(b, s, MAIN_COLS)
        dt, ac, act = _dt_prep(xf.reshape(b, s, d), g1, w_dt, dt_b, a_neg)

        def by_group(arr):
            return arr[:, :, :DT_COLS].reshape(b, s, SSM_GROUPS, DT_LANES_PER_GROUP).transpose(0, 2, 1, 3)

        act_g = act[:, :DT_COLS, :].reshape(b, SSM_GROUPS, DT_LANES_PER_GROUP, s)

        u = _conv_silu(proj3, conv_w[i], conv_b[i][None, :], 256)
        y_ssd = _ssd_scan(u, by_group(dt), by_group(ac), act_g, jnp.repeat(d_skip[i], SSM_HEAD_DIM)[None, :])

        gqk = jnp.tile(qk_norm_g[i], (1, 2))
        attn = _diff_attention(proj3, bias_win, gqk, lambda_qk[i], attn_head_norm_g[i][None, :], tq, lam_init)

        w_r = jnp.pad(jnp.concatenate([w_router_group[i], w_router_expert[i]], axis=1),
                      ((0, 0), (0, LANES - MOE_GROUPS - MOE_EXPERTS)))
        b_r = jnp.pad(jnp.concatenate([b_router_group[i], b_router_expert[i]]),
                      (0, LANES - MOE_GROUPS - MOE_EXPERTS))[None, :]
        x_mid, hn, ids, gates = _post_mixer(
            attn.reshape(t, ATTN_WIDTH), y_ssd.reshape(t, SSM_INNER), proj, xf, ssm_norm_g[i][None, :],
            w_attn_out[i].astype(BF16), w_ssm_out[i].astype(BF16), w_out[i].astype(BF16), norm2_g[i][None, :],
            w_r, b_r, _pick(t, 256))

        blk_expert, src_tok, nused, pos_kt = _dispatch_plan(ids[:, :MOE_TOPK])
        y_pad = _moe_experts(blk_expert, src_tok, nused, hn, w_exp_gate[i].astype(BF16), w_exp_up[i].astype(BF16),
                             w_exp_down[i].astype(BF16))
        xf = _moe_combine(pos_kt, x_mid, gates, y_pad, _pick(t, 128))
    return xf.reshape(b, s, d)
```

```python
import functools
import math

import jax
import jax.numpy as jnp
import numpy as np
from jax import lax
from jax.experimental import pallas as pl
from jax.experimental.pallas import tpu as pltpu

D_MODEL = 1024
DEPTH = 2
ATTN_HEADS = 8
ATTN_HEAD_DIM = 64
ATTN_V_DIM = 2 * ATTN_HEAD_DIM
ATTN_WIDTH = ATTN_HEADS * ATTN_V_DIM
NUM_BUCKETS = 32
MAX_DISTANCE = 128
SSM_INNER = 2 * D_MODEL
SSM_HEAD_DIM = 64
SSM_HEADS = SSM_INNER // SSM_HEAD_DIM
SSM_GROUPS = 4
SSM_HEADS_PER_GROUP = SSM_HEADS // SSM_GROUPS
SSM_STATE = 128
SSM_CONV = 5
SSM_CHUNK = 128
SSM_CONV_CH = SSM_INNER + 2 * SSM_GROUPS * SSM_STATE
MOE_GROUPS = 4
MOE_EXPERTS_PER_GROUP = 8
MOE_EXPERTS = MOE_GROUPS * MOE_EXPERTS_PER_GROUP
MOE_TOPK = 2
MOE_FF = 512
MOE_BLOCK = 128
RMS_EPS = 1e-6

LANES = 128
COL_Q = 0
COL_K = COL_Q + ATTN_WIDTH
COL_V = COL_K + ATTN_WIDTH
COL_Z = COL_V + ATTN_WIDTH
COL_XBC = COL_Z + SSM_INNER
MAIN_COLS = COL_XBC + SSM_CONV_CH
DT_COLS = 2 * SSM_HEADS
COL_GATES = MAIN_COLS + DT_COLS
DT_LANES_PER_GROUP = 2 * SSM_HEADS_PER_GROUP
PACKED = D_MODEL // 2

VMEM_LIMIT_BYTES = 56 * 1024 * 1024

F32 = jnp.float32
BF16 = jnp.bfloat16


def _params(*semantics):
    return pltpu.CompilerParams(dimension_semantics=semantics, vmem_limit_bytes=VMEM_LIMIT_BYTES)


def _rms(x, g):
    ms = jnp.mean(x * x, axis=-1, keepdims=True)
    return x * lax.rsqrt(ms + RMS_EPS) * g


def _norm_matmul_kernel(x_ref, g_ref, w_ref, o_ref, hn_ref):
    @pl.when(pl.program_id(1) == 0)
    def _():
        hn_ref[...] = _rms(x_ref[...], g_ref[...]).astype(hn_ref.dtype)

    o_ref[...] = jnp.dot(hn_ref[...], w_ref[...].astype(BF16), preferred_element_type=F32).astype(o_ref.dtype)


def _norm_matmul(x, g, w, layer, n, out_dtype, tm, tn):
    t, d = x.shape
    return pl.pallas_call(
        _norm_matmul_kernel,
        grid=(t // tm, n // tn),
        in_specs=[
            pl.BlockSpec((tm, d), lambda i, j: (i, 0)),
            pl.BlockSpec((1, d), lambda i, j: (0, 0)),
            pl.BlockSpec((None, d, tn), lambda i, j: (layer, 0, j)),
        ],
        out_specs=pl.BlockSpec((tm, tn), lambda i, j: (i, j)),
        out_shape=jax.ShapeDtypeStruct((t, n), out_dtype),
        scratch_shapes=[pltpu.VMEM((tm, d), BF16)],
        compiler_params=_params("parallel", "arbitrary"),
        name="in_proj",
    )(x, g, w)


def _dt_prep_kernel(x_ref, g_ref, w_ref, bias_ref, a_ref, dt_ref, ac_ref, act_ref):
    hn = _rms(x_ref[...], g_ref[...]).astype(BF16)
    raw = jnp.dot(hn, w_ref[...], preferred_element_type=F32) + bias_ref[...]
    dt = jnp.maximum(raw, 0.0) + jnp.log1p(jnp.exp(-jnp.abs(raw)))
    a = dt * a_ref[...]
    s = a.shape[0]
    pos = lax.broadcasted_iota(jnp.int32, a.shape, 0) & (SSM_CHUNK - 1)
    pre = a
    suf = a
    k = 1
    while k < SSM_CHUNK:
        pre = pre + jnp.where(pos >= k, pltpu.roll(pre, k, 0), 0.0)
        suf = suf + jnp.where(pos < SSM_CHUNK - k, pltpu.roll(suf, s - k, 0), 0.0)
        k *= 2
    lane = lax.broadcasted_iota(jnp.int32, a.shape, 1)
    is_fwd = (lane & (DT_LANES_PER_GROUP - 1)) < SSM_HEADS_PER_GROUP
    ac = jnp.where(is_fwd, pre, suf)
    dt_ref[...] = dt
    ac_ref[...] = ac
    act_ref[...] = ac.T


def _dt_prep(x3, g, w_dt, bias, a_neg):
    b, s, d = x3.shape
    out = jax.ShapeDtypeStruct((b, s, LANES), F32)
    out_t = jax.ShapeDtypeStruct((b, LANES, s), F32)
    vec = pl.BlockSpec((1, LANES), lambda i: (0, 0))
    return pl.pallas_call(
        _dt_prep_kernel,
        grid=(b,),
        in_specs=[
            pl.BlockSpec((None, s, d), lambda i: (i, 0, 0)),
            pl.BlockSpec((1, d), lambda i: (0, 0)),
            pl.BlockSpec((d, LANES), lambda i: (0, 0)),
            vec,
            vec,
        ],
        out_specs=[
            pl.BlockSpec((None, s, LANES), lambda i: (i, 0, 0)),
            pl.BlockSpec((None, s, LANES), lambda i: (i, 0, 0)),
            pl.BlockSpec((None, LANES, s), lambda i: (i, 0, 0)),
        ],
        out_shape=[out, out, out_t],
        compiler_params=_params("parallel"),
        name="dt_prep",
    )(x3, g, w_dt, bias, a_neg)


def _conv_kernel(x_ref, w_ref, b_ref, o_ref):
    xf = x_ref[...].astype(F32)
    s = xf.shape[0]
    row = lax.broadcasted_iota(jnp.int32, xf.shape, 0)
    pad = SSM_CONV // 2
    acc = xf * w_ref[pad:pad + 1, :] + b_ref[...]
    for j in range(SSM_CONV):
        shift = pad - j
        if shift == 0:
            continue
        rolled = pltpu.roll(xf, shift % s, 0)
        valid = (row >= shift) & (row < s + shift)
        acc = acc + jnp.where(valid, rolled, 0.0) * w_ref[j:j + 1, :]
    o_ref[...] = (acc * jax.nn.sigmoid(acc)).astype(o_ref.dtype)


def _conv_silu(proj3, conv_w, conv_b, tc):
    b, s, _ = proj3.shape
    col0 = COL_XBC // tc
    return pl.pallas_call(
        _conv_kernel,
        grid=(b, SSM_CONV_CH // tc),
        in_specs=[
            pl.BlockSpec((None, s, tc), lambda i, c: (i, 0, col0 + c)),
            pl.BlockSpec((SSM_CONV, tc), lambda i, c: (0, c)),
            pl.BlockSpec((1, tc), lambda i, c: (0, c)),
        ],
        out_specs=pl.BlockSpec((None, s, tc), lambda i, c: (i, 0, c)),
        out_shape=jax.ShapeDtypeStruct((b, s, SSM_CONV_CH), BF16),
        compiler_params=_params("parallel", "parallel"),
        name="conv_silu",
    )(proj3, conv_w, conv_b)


def _rel_bucket(rel):
    half = NUM_BUCKETS // 2
    max_exact = half // 2
    ret = jnp.where(rel > 0, half, 0)
    n = jnp.abs(rel)
    n_safe = jnp.maximum(n, 1).astype(F32)
    large = max_exact + (jnp.log(n_safe / max_exact) / math.log(MAX_DISTANCE / max_exact)
                         * (half - max_exact)).astype(jnp.int32)
    large = jnp.minimum(large, half - 1)
    return ret + jnp.where(n < max_exact, n, large)


def _bias_kernel(relb_ref, bid_ref, o_ref, *, tq):
    h = pl.program_id(0)
    bid = bid_ref[...]
    fv = jnp.zeros(bid.shape, F32)
    for bkt in range(NUM_BUCKETS):
        fv = jnp.where(bid == bkt, relb_ref[h, bkt], fv)
    g = jnp.broadcast_to(fv, o_ref.shape)
    row = lax.broadcasted_iota(jnp.int32, o_ref.shape, 0)
    k = 1
    while k < tq:
        g = jnp.where((row & k) != 0, pltpu.roll(g, k, 1), g)
        k *= 2
    o_ref[...] = g


def _bias_windows(rel_bias, s, tq):
    width = 2 * s
    m = jnp.arange(width, dtype=jnp.int32)
    bid = _rel_bucket(((m + tq) % width) - s).astype(jnp.int32)[None, :]
    return pl.pallas_call(
        functools.partial(_bias_kernel, tq=tq),
        grid=(ATTN_HEADS,),
        in_specs=[
            pl.BlockSpec(memory_space=pltpu.SMEM),
            pl.BlockSpec((1, width), lambda h: (0, 0)),
        ],
        out_specs=pl.BlockSpec((None, tq, width), lambda h: (h, 0, 0)),
        out_shape=jax.ShapeDtypeStruct((ATTN_HEADS, tq, width), F32),
        compiler_params=_params("parallel"),
        name="rel_bias_windows",
    )(rel_bias.T.astype(F32), bid)


SOFTMAX_SHIFT_SLACK = 40.0


def _attn_kernel(relb_ref, q_ref, k_ref, v_ref, bias_ref, gqk_ref, lamqk_ref, gh_ref, o_ref, q0_ref, q1_ref, kn_ref,
                 vx_ref, shift_ref, slack_ref, *, tq, lam_init):
    h = pl.program_id(0)
    qi = pl.program_id(2)
    s = k_ref.shape[0]
    lo = lax.broadcasted_iota(jnp.int32, (1, LANES), 1) < ATTN_HEAD_DIM

    def half_sums(x2):
        s_all = jnp.sum(x2, axis=-1, keepdims=True)
        s_lo = jnp.sum(jnp.where(lo, x2, 0.0), axis=-1, keepdims=True)
        return s_lo, jnp.maximum(s_all - s_lo, 0.0)

    def half_norm(x, g):
        s_lo, s_hi = half_sums(x * x)
        ms = jnp.where(lo, s_lo, s_hi) * (1.0 / ATTN_HEAD_DIM)
        return x * lax.rsqrt(ms + RMS_EPS) * g

    @pl.when(qi == 0)
    def _():
        kn = half_norm(k_ref[...].astype(F32), gqk_ref[1:2, :]).astype(BF16)
        kn_ref[...] = kn
        k_lo, k_hi = half_sums(jnp.square(kn.astype(F32)))
        kmax = jnp.where(lo, jnp.sqrt(jnp.max(k_lo, axis=0, keepdims=True)),
                         jnp.sqrt(jnp.max(k_hi, axis=0, keepdims=True)))
        vx_ref[:, :LANES] = v_ref[...]
        vx_ref[:, LANES:] = jnp.ones((s, LANES), BF16)
        bmax = relb_ref[h, 0]
        bmin = relb_ref[h, 0]
        for bkt in range(1, NUM_BUCKETS):
            bmax = jnp.maximum(bmax, relb_ref[h, bkt])
            bmin = jnp.minimum(bmin, relb_ref[h, bkt])
        qn = (half_norm(q_ref[...].astype(F32), gqk_ref[0:1, :]) * (ATTN_HEAD_DIM ** -0.5)).astype(BF16)
        q0_ref[...] = jnp.where(lo, qn, jnp.zeros_like(qn))
        q1_ref[...] = jnp.where(lo, jnp.zeros_like(qn), qn)
        q_lo, q_hi = half_sums(jnp.square(qn.astype(F32)))
        reach = jnp.sqrt(jnp.where(lo, q_lo, q_hi)) * kmax
        shift_ref[...] = reach + bmax
        slack_ref[0] = 2.0 * jnp.max(reach) + (bmax - bmin)

    rows = pl.ds(pl.multiple_of(qi * tq, tq), tq)
    q0 = q0_ref[rows, :]
    q1 = q1_ref[rows, :]
    slack = slack_ref[0]
    c0 = pl.multiple_of(s - (qi + 1) * tq, tq)
    lq = lamqk_ref[...]
    lam = (jnp.exp(jnp.sum(lq[0:1] * lq[1:2], axis=-1, keepdims=True))
           - jnp.exp(jnp.sum(lq[2:3] * lq[3:4], axis=-1, keepdims=True)) + lam_init)

    def logits_of(qm):
        return lax.dot_general(qm, kn_ref[...], (((1,), (1,)), ((), ())),
                               preferred_element_type=F32) + bias_ref[:, pl.ds(c0, s)]

    def finish(out):
        o_ref[...] = (_rms(out, gh_ref[...]) * (1.0 - lam_init)).astype(o_ref.dtype)

    @pl.when(slack <= SOFTMAX_SHIFT_SLACK)
    def _():
        shift = shift_ref[rows, :]

        def weighted_v(qm, m):
            p = jnp.exp(logits_of(qm) - m).astype(BF16)
            return jnp.dot(p, vx_ref[...], preferred_element_type=F32)

        o0 = weighted_v(q0, shift[:, 0:1])
        o1 = weighted_v(q1, shift[:, ATTN_HEAD_DIM:ATTN_HEAD_DIM + 1])
        finish(o0[:, :LANES] / o0[:, LANES:LANES + 1] - o1[:, :LANES] * (lam / o1[:, LANES:LANES + 1]))

    @pl.when(slack > SOFTMAX_SHIFT_SLACK)
    def _():
        def softmax_map(qm):
            logits = logits_of(qm)
            p = jnp.exp(logits - jnp.max(logits, axis=-1, keepdims=True))
            return p, jnp.sum(p, axis=-1, keepdims=True)

        p0, l0 = softmax_map(q0)
        p1, l1 = softmax_map(q1)
        attn = p0 * (1.0 / l0) - p1 * (lam / l1)
        finish(jnp.dot(attn.astype(BF16), v_ref[...], preferred_element_type=F32))


def _diff_attention(proj3, bias_win, relb_t, gqk, lamqk, gh, tq, lam_init):
    b, s, _ = proj3.shape
    width = bias_win.shape[-1]
    return pl.pallas_call(
        functools.partial(_attn_kernel, tq=tq, lam_init=lam_init),
        grid=(ATTN_HEADS, b, s // tq),
        in_specs=[
            pl.BlockSpec(memory_space=pltpu.SMEM),
            pl.BlockSpec((None, s, LANES), lambda h, i, q: (i, 0, COL_Q // LANES + h)),
            pl.BlockSpec((None, s, LANES), lambda h, i, q: (i, 0, COL_K // LANES + h)),
            pl.BlockSpec((None, s, LANES), lambda h, i, q: (i, 0, COL_V // LANES + h)),
            pl.BlockSpec((None, tq, width), lambda h, i, q: (h, 0, 0)),
            pl.BlockSpec((2, LANES), lambda h, i, q: (0, 0)),
            pl.BlockSpec((4, ATTN_HEAD_DIM), lambda h, i, q: (0, 0)),
            pl.BlockSpec((1, LANES), lambda h, i, q: (0, 0)),
        ],
        out_specs=pl.BlockSpec((None, tq, LANES), lambda h, i, q: (i, q, h)),
        out_shape=jax.ShapeDtypeStruct((b, s, ATTN_WIDTH), BF16),
        scratch_shapes=[pltpu.VMEM((s, LANES), BF16), pltpu.VMEM((s, LANES), BF16), pltpu.VMEM((s, LANES), BF16),
                        pltpu.VMEM((s, 2 * LANES), BF16), pltpu.VMEM((s, LANES), F32), pltpu.SMEM((1,), F32)],
        compiler_params=_params("parallel", "parallel", "arbitrary"),
        name="diff_attention",
    )(relb_t, proj3, proj3, proj3, bias_win, gqk, lamqk, gh)


def _ssd_kernel(x_ref, b_ref, c_ref, dt_ref, ac_ref, act_ref, dsk_ref, y_ref, st_ref, acc_ref):
    L = SSM_CHUNK
    s = x_ref.shape[0]
    n_chunks = s // L
    row = lax.broadcasted_iota(jnp.int32, (L, L), 0)
    col = lax.broadcasted_iota(jnp.int32, (L, L), 1)
    lo = lax.broadcasted_iota(jnp.int32, (L, LANES), 1) < SSM_HEAD_DIM
    lo1 = lo[0:1, :]
    n_pairs = SSM_HEADS_PER_GROUP // 2

    def one_chunk(c, backward):
        r0 = pl.multiple_of(c * L, L)
        off = SSM_HEADS_PER_GROUP if backward else 0
        tot_row = 0 if backward else L - 1
        mask = (row <= col) if backward else (row >= col)
        xc = x_ref[pl.ds(r0, L), :].astype(F32)
        bc = b_ref[pl.ds(r0, L), :]
        cc = c_ref[pl.ds(r0, L), :]
        dtc = dt_ref[pl.ds(r0, L), :]
        acc = ac_ref[pl.ds(r0, L), :]
        act = act_ref[:, pl.ds(r0, L)]
        cb = lax.dot_general(cc, bc, (((1,), (1,)), ((), ())), preferred_element_type=F32)
        y_off = jnp.dot(cc, st_ref[...].astype(BF16), preferred_element_type=F32)

        def col_of(arr, e):
            return jnp.broadcast_to(arr[:, off + e:off + e + 1], (L, LANES))

        ys, xdecs, tots = [], [], []
        for j in range(n_pairs):
            e0, e1 = 2 * j, 2 * j + 1
            dt_pair = jnp.where(lo, col_of(dtc, e0), col_of(dtc, e1))
            ac_pair = jnp.where(lo, col_of(acc, e0), col_of(acc, e1))
            tot_pair = jnp.where(lo1, acc[tot_row:tot_row + 1, off + e0:off + e0 + 1],
                                 acc[tot_row:tot_row + 1, off + e1:off + e1 + 1])
            xdt = xc[:, j * LANES:(j + 1) * LANES] * dt_pair
            xdt_b = xdt.astype(BF16)
            diag = []
            for e in (e0, e1):
                decay = jnp.where(mask, jnp.exp(col_of(acc, e) - act[off + e:off + e + 1, :]), 0.0)
                diag.append(jnp.dot((cb * decay).astype(BF16), xdt_b, preferred_element_type=F32))
            ys.append(jnp.where(lo, diag[0], diag[1]) + y_off[:, j * LANES:(j + 1) * LANES] * jnp.exp(ac_pair))
            xdecs.append((xdt * jnp.exp(tot_pair - ac_pair)).astype(BF16))
            tots.append(jnp.exp(tot_pair))
        y = jnp.concatenate(ys, axis=1)
        xdec = jnp.concatenate(xdecs, axis=1)
        st_new = lax.dot_general(bc, xdec, (((0,), (0,)), ((), ())), preferred_element_type=F32)
        st_ref[...] = st_ref[...] * jnp.concatenate(tots, axis=1) + st_new
        if backward:
            y_ref[pl.ds(r0, L), :] = (acc_ref[pl.ds(r0, L), :] + y).astype(y_ref.dtype)
        else:
            acc_ref[pl.ds(r0, L), :] = y + xc * dsk_ref[...]

    st_ref[...] = jnp.zeros_like(st_ref)

    def fwd_body(c, carry):
        one_chunk(c, False)
        return carry

    lax.fori_loop(0, n_chunks, fwd_body, 0)
    st_ref[...] = jnp.zeros_like(st_ref)

    def bwd_body(c, carry):
        one_chunk(n_chunks - 1 - c, True)
        return carry

    lax.fori_loop(0, n_chunks, bwd_body, 0)


def _ssd_scan(u, dt_g, ac_g, act_g, dskip):
    b, s, _ = u.shape
    gw = SSM_HEADS_PER_GROUP * SSM_HEAD_DIM
    xcols = SSM_INNER // SSM_STATE
    nl = DT_LANES_PER_GROUP
    return pl.pallas_call(
        _ssd_kernel,
        grid=(b, SSM_GROUPS),
        in_specs=[
            pl.BlockSpec((None, s, gw), lambda i, g: (i, 0, g)),
            pl.BlockSpec((None, s, SSM_STATE), lambda i, g: (i, 0, xcols + g)),
            pl.BlockSpec((None, s, SSM_STATE), lambda i, g: (i, 0, xcols + SSM_GROUPS + g)),
            pl.BlockSpec((None, None, s, nl), lambda i, g: (i, g, 0, 0)),
            pl.BlockSpec((None, None, s, nl), lambda i, g: (i, g, 0, 0)),
            pl.BlockSpec((None, None, nl, s), lambda i, g: (i, g, 0, 0)),
            pl.BlockSpec((1, gw), lambda i, g: (0, g)),
        ],
        out_specs=pl.BlockSpec((None, s, gw), lambda i, g: (i, 0, g)),
        out_shape=jax.ShapeDtypeStruct((b, s, SSM_INNER), BF16),
        scratch_shapes=[pltpu.VMEM((SSM_STATE, gw), F32), pltpu.VMEM((s, gw), F32)],
        compiler_params=_params("parallel", "parallel"),
        name="ssd_scan",
    )(u, u, u, dt_g, ac_g, act_g, dskip)


def _route(logits):
    lane = lax.broadcasted_iota(jnp.int32, logits.shape, 1)
    lane_f = lane.astype(F32)
    neg = -jnp.inf
    big = float(LANES)
    is_g = lane < MOE_GROUPS
    gl = jnp.where(is_g, logits, neg)
    gmax = jnp.max(gl, axis=-1, keepdims=True)
    g_idx = jnp.min(jnp.where(gl == gmax, lane_f, big), axis=-1, keepdims=True)
    g_gate = 1.0 / jnp.sum(jnp.where(is_g, jnp.exp(logits - gmax), 0.0), axis=-1, keepdims=True)
    first = MOE_GROUPS + MOE_EXPERTS_PER_GROUP * g_idx
    in_grp = (lane_f >= first) & (lane_f < first + MOE_EXPERTS_PER_GROUP)
    el = jnp.where(in_grp, logits, neg)
    v1 = jnp.max(el, axis=-1, keepdims=True)
    i1 = jnp.min(jnp.where(el == v1, lane_f, big), axis=-1, keepdims=True)
    el2 = jnp.where(lane_f == i1, neg, el)
    v2 = jnp.max(el2, axis=-1, keepdims=True)
    i2 = jnp.min(jnp.where(el2 == v2, lane_f, big), axis=-1, keepdims=True)
    r = jnp.exp(v2 - v1)
    p1 = 1.0 / (1.0 + r)
    p2 = r * p1
    ids = jnp.where(lane == 0, i1 - MOE_GROUPS, jnp.where(lane == 1, i2 - MOE_GROUPS, 0.0)).astype(jnp.int32)
    gates = jnp.where(lane == 0, p1 * g_gate, jnp.where(lane == 1, p2 * g_gate, 0.0))
    return ids, gates


_HI_HALF = 0xFFFF0000


def _pack_bf16_pairs(x):
    n = x.shape[1] // 2
    bits = lax.bitcast_convert_type(x.astype(BF16).astype(F32), jnp.uint32)
    return (bits[:, n:] & jnp.uint32(_HI_HALF)) | (bits[:, :n] >> 16)


def _unpack_bf16_pairs(w):
    lo = lax.bitcast_convert_type(w << 16, F32)
    hi = lax.bitcast_convert_type(w & jnp.uint32(_HI_HALF), F32)
    return jnp.concatenate([lo, hi], axis=1)


def _post_kernel(attn_ref, y_ref, z0_ref, z1_ref, ga_ref, gm_ref, x_ref, gs_ref, wa_ref, ws_ref, wo_ref, g2_ref,
                 wr_ref, br_ref, xo_ref, hn_ref, id_ref, gate_ref):
    ya = jnp.dot(attn_ref[...], wa_ref[...], preferred_element_type=F32)
    z = jnp.concatenate([z0_ref[...], z1_ref[...]], axis=1).astype(F32)
    gated = y_ref[...].astype(F32) * (z * jax.nn.sigmoid(z))
    ym = jnp.dot(_rms(gated, gs_ref[...]).astype(BF16), ws_ref[...], preferred_element_type=F32)
    merged = (jax.nn.sigmoid(ga_ref[...].astype(F32)) * ya + jax.nn.sigmoid(gm_ref[...].astype(F32)) * ym)
    x_new = x_ref[...] + jnp.dot(merged.astype(BF16), wo_ref[...], preferred_element_type=F32)
    xo_ref[...] = x_new
    hn = _rms(x_new, g2_ref[...])
    hn_ref[...] = _pack_bf16_pairs(hn)
    logits = jnp.dot(hn, wr_ref[...], preferred_element_type=F32, precision=lax.Precision.HIGHEST) + br_ref[...]
    ids, gates = _route(logits)
    id_ref[...] = ids
    gate_ref[...] = gates


def _post_mixer(attn, y, proj, gate_proj, x, gs, wa, ws, wo, g2, wr, br, tm):
    t, d = x.shape
    row = lambda w, c: pl.BlockSpec((tm, w), lambda i: (i, c))
    full = lambda a: pl.BlockSpec(a.shape, lambda i: (0,) * a.ndim)
    return pl.pallas_call(
        _post_kernel,
        grid=(t // tm,),
        in_specs=[
            row(ATTN_WIDTH, 0),
            row(SSM_INNER, 0),
            row(D_MODEL, COL_Z // D_MODEL),
            row(D_MODEL, COL_Z // D_MODEL + 1),
            row(D_MODEL, 0),
            row(D_MODEL, 1),
            row(D_MODEL, 0),
            full(gs), full(wa), full(ws), full(wo), full(g2), full(wr), full(br),
        ],
        out_specs=[row(D_MODEL, 0), row(PACKED, 0), row(LANES, 0), row(LANES, 0)],
        out_shape=[
            jax.ShapeDtypeStruct((t, d), F32),
            jax.ShapeDtypeStruct((t, PACKED), jnp.uint32),
            jax.ShapeDtypeStruct((t, LANES), jnp.int32),
            jax.ShapeDtypeStruct((t, LANES), F32),
        ],
        compiler_params=_params("parallel"),
        name="post_mixer",
    )(attn, y, proj, proj, gate_proj, gate_proj, x, gs, wa, ws, wo, g2, wr, br)


GATHER_DEPTH = 3


def _start_rows(idx_ref, base, src_hbm, dst_ref, sem):
    for r in range(dst_ref.shape[0]):
        pltpu.make_async_copy(src_hbm.at[idx_ref[base + r]], dst_ref.at[pl.ds(r, 1), :], sem).start()


def _wait_buffer(dst_ref, sem):
    pltpu.make_async_copy(dst_ref, dst_ref, sem).wait()


def _moe_kernel(blk_e_ref, src_ref, nused_ref, hn_hbm, wg_ref, wu_ref, wd_ref, y_ref, xg_ref, wgb_ref, wub_ref,
                wdb_ref, sem):
    i = pl.program_id(0)
    nused = nused_ref[0]

    def start(blk):
        slot = blk % GATHER_DEPTH
        _start_rows(src_ref, blk * MOE_BLOCK, hn_hbm, xg_ref.at[slot], sem.at[slot])

    @pl.when(i == 0)
    def _():
        start(0)

    @pl.when((i == 0) & (nused > 1))
    def _():
        start(1)

    new_expert = (i == 0) | (blk_e_ref[i] != blk_e_ref[jnp.maximum(i - 1, 0)])

    @pl.when((i < nused) & new_expert)
    def _():
        wgb_ref[...] = wg_ref[...].astype(BF16)
        wub_ref[...] = wu_ref[...].astype(BF16)
        wdb_ref[...] = wd_ref[...].astype(BF16)

    def compute():
        xb = _unpack_bf16_pairs(xg_ref[i % GATHER_DEPTH]).astype(BF16)
        hg = jnp.dot(xb, wgb_ref[...], preferred_element_type=F32)
        hu = jnp.dot(xb, wub_ref[...], preferred_element_type=F32)
        hid = (hg * jax.nn.sigmoid(hg) * hu).astype(BF16)
        y_ref[...] = _pack_bf16_pairs(jnp.dot(hid, wdb_ref[...], preferred_element_type=F32))

    def wait():
        slot = i % GATHER_DEPTH
        _wait_buffer(xg_ref.at[slot], sem.at[slot])

    @pl.when(i + 2 < nused)
    def _():
        wait()
        start(i + 2)
        compute()

    @pl.when((i < nused) & (i + 2 >= nused))
    def _():
        wait()
        compute()

    @pl.when(i >= nused)
    def _():
        y_ref[...] = jnp.zeros_like(y_ref)


def _moe_experts(blk_expert, src_tok, nused, hn3, wg, wu, wd, layer):
    n_blocks = blk_expert.shape[0]
    d = wg.shape[2]
    grid_spec = pltpu.PrefetchScalarGridSpec(
        num_scalar_prefetch=3,
        grid=(n_blocks,),
        in_specs=[
            pl.BlockSpec(memory_space=pl.ANY),
            pl.BlockSpec((None, None, d, MOE_FF), lambda i, be, st, nu: (layer, be[i], 0, 0)),
            pl.BlockSpec((None, None, d, MOE_FF), lambda i, be, st, nu: (layer, be[i], 0, 0)),
            pl.BlockSpec((None, None, MOE_FF, d), lambda i, be, st, nu: (layer, be[i], 0, 0)),
        ],
        out_specs=pl.BlockSpec((MOE_BLOCK, PACKED), lambda i, be, st, nu: (i, 0)),
        scratch_shapes=[
            pltpu.VMEM((GATHER_DEPTH, MOE_BLOCK, PACKED), jnp.uint32),
            pltpu.VMEM((d, MOE_FF), BF16),
            pltpu.VMEM((d, MOE_FF), BF16),
            pltpu.VMEM((MOE_FF, d), BF16),
            pltpu.SemaphoreType.DMA((GATHER_DEPTH,)),
        ],
    )
    return pl.pallas_call(
        _moe_kernel,
        grid_spec=grid_spec,
        out_shape=jax.ShapeDtypeStruct((n_blocks * MOE_BLOCK, PACKED), jnp.uint32),
        compiler_params=_params("arbitrary"),
        name="moe_experts",
    )(blk_expert, src_tok, nused, hn3, wg, wu, wd)


def _combine_kernel(pos_ref, x_ref, gate_ref, y_hbm, o_ref, yg_ref, sem, *, tt, n_tiles):
    i = pl.program_id(0)

    def start(tile):
        slot = tile % GATHER_DEPTH
        for k in range(MOE_TOPK):
            _start_rows(pos_ref, (k * n_tiles + tile) * tt, y_hbm, yg_ref.at[slot, k], sem.at[slot])

    def finish():
        gates = gate_ref[...]
        slot = i % GATHER_DEPTH
        o_ref[...] = (x_ref[...] + gates[:, 0:1] * _unpack_bf16_pairs(yg_ref[slot, 0])
                      + gates[:, 1:2] * _unpack_bf16_pairs(yg_ref[slot, 1]))

    def wait():
        slot = i % GATHER_DEPTH
        _wait_buffer(yg_ref.at[slot], sem.at[slot])

    @pl.when(i == 0)
    def _():
        start(0)
        if n_tiles > 1:
            start(1)

    @pl.when(i + 2 < n_tiles)
    def _():
        wait()
        start(i + 2)
        finish()

    @pl.when(i + 2 >= n_tiles)
    def _():
        wait()
        finish()


def _moe_combine(pos_kt, x, gates, y3, tt):
    t, d = x.shape
    n_tiles = t // tt
    grid_spec = pltpu.PrefetchScalarGridSpec(
        num_scalar_prefetch=1,
        grid=(n_tiles,),
        in_specs=[
            pl.BlockSpec((tt, d), lambda i, p: (i, 0)),
            pl.BlockSpec((tt, LANES), lambda i, p: (i, 0)),
            pl.BlockSpec(memory_space=pl.ANY),
        ],
        out_specs=pl.BlockSpec((tt, d), lambda i, p: (i, 0)),
        scratch_shapes=[pltpu.VMEM((GATHER_DEPTH, MOE_TOPK, tt, PACKED), jnp.uint32),
                        pltpu.SemaphoreType.DMA((GATHER_DEPTH,))],
    )
    return pl.pallas_call(
        functools.partial(_combine_kernel, tt=tt, n_tiles=n_tiles),
        grid_spec=grid_spec,
        out_shape=jax.ShapeDtypeStruct((t, d), F32),
        compiler_params=_params("arbitrary"),
        name="moe_combine",
    )(pos_kt, x, gates, y3)


def _plan_kernel(ids_ref, rank_ref, cnt_ref, carry_ref):
    @pl.when(pl.program_id(0) == 0)
    def _():
        carry_ref[...] = jnp.zeros_like(carry_ref)

    ids = ids_ref[...]
    tt = ids.shape[0]
    lane = lax.broadcasted_iota(jnp.int32, ids.shape, 1)
    oh0 = jnp.where(lane == ids[:, 0:1], 1.0, 0.0)
    oh1 = jnp.where(lane == ids[:, 1:2], 1.0, 0.0)
    earlier = jnp.where(lax.broadcasted_iota(jnp.int32, (tt, tt), 1) < lax.broadcasted_iota(jnp.int32, (tt, tt), 0),
                        1.0, 0.0).astype(BF16)
    before0 = jnp.dot(earlier, oh0.astype(BF16), preferred_element_type=F32)
    before1 = jnp.dot(earlier, oh1.astype(BF16), preferred_element_type=F32)
    tot0 = jnp.sum(oh0, axis=0, keepdims=True)
    carry = carry_ref[...]
    rank0 = jnp.sum(oh0 * (before0 + carry), axis=-1, keepdims=True)
    rank1 = jnp.sum(oh1 * (before1 + tot0 + carry), axis=-1, keepdims=True)
    carry = carry + tot0 + jnp.sum(oh1, axis=0, keepdims=True)
    carry_ref[...] = carry
    cnt_ref[...] = carry
    rank_ref[...] = jnp.where(lane == 0, rank0, jnp.where(lane == 1, rank1, 0.0)).astype(jnp.int32)


def _dispatch_plan(ids, tt):
    t = ids.shape[0]
    n_assign = t * MOE_TOPK
    rank, cnt = pl.pallas_call(
        _plan_kernel,
        grid=(t // tt,),
        in_specs=[pl.BlockSpec((tt, LANES), lambda i: (i, 0))],
        out_specs=[pl.BlockSpec((tt, LANES), lambda i: (i, 0)), pl.BlockSpec((1, LANES), lambda i: (0, 0))],
        out_shape=[jax.ShapeDtypeStruct((t, LANES), jnp.int32), jax.ShapeDtypeStruct((1, LANES), F32)],
        scratch_shapes=[pltpu.VMEM((1, LANES), F32)],
        compiler_params=_params("arbitrary"),
        name="moe_plan",
    )(ids)
    counts = cnt[0, :MOE_EXPERTS].astype(jnp.int32)
    padded = (counts + MOE_BLOCK - 1) // MOE_BLOCK * MOE_BLOCK
    pad_end = jnp.cumsum(padded)
    pad_start = pad_end - padded
    e_ids = ids[:, :MOE_TOPK]
    onehot = e_ids[:, :, None] == jnp.arange(MOE_EXPERTS, dtype=jnp.int32)[None, None, :]
    dest = jnp.sum(jnp.where(onehot, pad_start[None, None, :], 0), axis=-1) + rank[:, :MOE_TOPK]
    n_blocks = -(-n_assign // MOE_BLOCK) + MOE_EXPERTS
    blk_expert = jnp.minimum(
        jnp.searchsorted(pad_end, jnp.arange(n_blocks, dtype=jnp.int32) * MOE_BLOCK, side="right"),
        MOE_EXPERTS - 1).astype(jnp.int32)
    tok = jnp.arange(n_assign, dtype=jnp.int32) // MOE_TOPK
    src_tok = jnp.zeros((n_blocks * MOE_BLOCK,), jnp.int32).at[dest.reshape(-1)].set(tok)
    nused = (pad_end[-1:] // MOE_BLOCK).astype(jnp.int32)
    pos_kt = dest.T.reshape(-1).astype(jnp.int32)
    return blk_expert, src_tok, nused, pos_kt


def _pick(n, pref):
    return pref if n % pref == 0 else n


def kernel(x, norm1_g, w_in, qk_norm_g, lambda_qk, attn_head_norm_g, rel_bias, conv_w, conv_b, dt_bias, a_log,
           d_skip, ssm_norm_g, w_attn_out, w_ssm_out, w_out, norm2_g, w_router_group, b_router_group,
           w_router_expert, b_router_expert, w_exp_gate, w_exp_up, w_exp_down):
    b, s, d = x.shape
    t = b * s
    tq = _pick(s, 256)
    bias_win = _bias_windows(rel_bias, s, tq)

    lane = np.arange(DT_COLS)
    grp, rem = lane // DT_LANES_PER_GROUP, lane % DT_LANES_PER_GROUP
    direction, e = rem // SSM_HEADS_PER_GROUP, rem % SSM_HEADS_PER_GROUP
    dt_src = direction * SSM_HEADS + grp * SSM_HEADS_PER_GROUP + e
    pad_lanes = LANES - DT_COLS

    xf = x.reshape(t, d)
    for i in range(DEPTH):
        lam_init = 0.8 - 0.6 * math.exp(-0.3 * i)
        g1 = norm1_g[i][None, :]
        w_dt = jnp.pad(w_in[i][:, MAIN_COLS + dt_src], ((0, 0), (0, pad_lanes))).astype(BF16)
        dt_b = jnp.pad(dt_bias[i].reshape(-1)[dt_src], (0, pad_lanes))[None, :]
        a_neg = jnp.pad(-jnp.exp(a_log[i].astype(F32)).reshape(-1)[dt_src], (0, pad_lanes))[None, :]

        tm_proj = _pick(t, 1024)
        proj = _norm_matmul(xf, g1, w_in, i, MAIN_COLS, BF16, tm_proj, 1024)
        gate_proj = _norm_matmul(xf, g1, w_in[i][None, :, COL_GATES:], 0, 2 * D_MODEL, BF16, tm_proj, 1024)
        proj3 = proj.reshape(b, s, MAIN_COLS)
        dt, ac, act = _dt_prep(xf.reshape(b, s, d), g1, w_dt, dt_b, a_neg)

        def by_group(arr):
            return arr[:, :, :DT_COLS].reshape(b, s, SSM_GROUPS, DT_LANES_PER_GROUP).transpose(0, 2, 1, 3)

        act_g = act[:, :DT_COLS, :].reshape(b, SSM_GROUPS, DT_LANES_PER_GROUP, s)

        u = _conv_silu(proj3, conv_w[i], conv_b[i][None, :], 256)
        y_ssd = _ssd_scan(u, by_group(dt), by_group(ac), act_g, jnp.repeat(d_skip[i], SSM_HEAD_DIM)[None, :])

        gqk = jnp.tile(qk_norm_g[i], (1, 2))
        attn = _diff_attention(proj3, bias_win, rel_bias.T.astype(F32), gqk, lambda_qk[i],
                               attn_head_norm_g[i][None, :], tq, lam_init)

        w_r = jnp.pad(jnp.concatenate([w_router_group[i], w_router_expert[i]], axis=1),
                      ((0, 0), (0, LANES - MOE_GROUPS - MOE_EXPERTS)))
        b_r = jnp.pad(jnp.concatenate([b_router_group[i], b_router_expert[i]]),
                      (0, LANES - MOE_GROUPS - MOE_EXPERTS))[None, :]
        x_mid, hn, ids, gates = _post_mixer(
            attn.reshape(t, ATTN_WIDTH), y_ssd.reshape(t, SSM_INNER), proj, gate_proj, xf, ssm_norm_g[i][None, :],
            w_attn_out[i].astype(BF16), w_ssm_out[i].astype(BF16), w_out[i].astype(BF16), norm2_g[i][None, :],
            w_r, b_r, _pick(t, 256))

        blk_expert, src_tok, nused, pos_kt = _dispatch_plan(ids, _pick(t, 256))
        y_pad = _moe_experts(blk_expert, src_tok, nused, hn.reshape(t, 1, PACKED), w_exp_gate, w_exp_up, w_exp_down, i)
        xf = _moe_combine(pos_kt, x_mid, gates, y_pad.reshape(-1, 1, PACKED), _pick(t, 128))
    return xf.reshape(b, s, d)
```

```python
import functools
import math

import jax
import jax.numpy as jnp
import numpy as np
from jax import lax
from jax.experimental import pallas as pl
from jax.experimental.pallas import tpu as pltpu

D_MODEL = 1024
DEPTH = 2
ATTN_HEADS = 8
ATTN_HEAD_DIM = 64
ATTN_V_DIM = 2 * ATTN_HEAD_DIM
ATTN_WIDTH = ATTN_HEADS * ATTN_V_DIM
NUM_BUCKETS = 32
MAX_DISTANCE = 128
SSM_INNER = 2 * D_MODEL
SSM_HEAD_DIM = 64
SSM_HEADS = SSM_INNER // SSM_HEAD_DIM
SSM_GROUPS = 4
SSM_HEADS_PER_GROUP = SSM_HEADS // SSM_GROUPS
SSM_STATE = 128
SSM_CONV = 5
SSM_CHUNK = 128
SSM_CONV_CH = SSM_INNER + 2 * SSM_GROUPS * SSM_STATE
MOE_GROUPS = 4
MOE_EXPERTS_PER_GROUP = 8
MOE_EXPERTS = MOE_GROUPS * MOE_EXPERTS_PER_GROUP
MOE_TOPK = 2
MOE_FF = 512
MOE_BLOCK = 128
RMS_EPS = 1e-6

LANES = 128
COL_Q = 0
COL_K = COL_Q + ATTN_WIDTH
COL_V = COL_K + ATTN_WIDTH
COL_Z = COL_V + ATTN_WIDTH
COL_XBC = COL_Z + SSM_INNER
MAIN_COLS = COL_XBC + SSM_CONV_CH
DT_COLS = 2 * SSM_HEADS
COL_GATES = MAIN_COLS + DT_COLS
DT_LANES_PER_GROUP = 2 * SSM_HEADS_PER_GROUP
PACKED = D_MODEL // 2

VMEM_LIMIT_BYTES = 56 * 1024 * 1024

F32 = jnp.float32
BF16 = jnp.bfloat16


def _params(*semantics):
    return pltpu.CompilerParams(dimension_semantics=semantics, vmem_limit_bytes=VMEM_LIMIT_BYTES)


def _rms(x, g):
    ms = jnp.mean(x * x, axis=-1, keepdims=True)
    return x * lax.rsqrt(ms + RMS_EPS) * g


def _norm_matmul_kernel(x_ref, g_ref, w_ref, o_ref, hn_ref):
    @pl.when(pl.program_id(1) == 0)
    def _():
        hn_ref[...] = _rms(x_ref[...], g_ref[...]).astype(hn_ref.dtype)

    o_ref[...] = jnp.dot(hn_ref[...], w_ref[...].astype(BF16), preferred_element_type=F32).astype(o_ref.dtype)


def _norm_matmul(x, g, w, layer, n, out_dtype, tm, tn):
    t, d = x.shape
    return pl.pallas_call(
        _norm_matmul_kernel,
        grid=(t // tm, n // tn),
        in_specs=[
            pl.BlockSpec((tm, d), lambda i, j: (i, 0)),
            pl.BlockSpec((1, d), lambda i, j: (0, 0)),
            pl.BlockSpec((None, d, tn), lambda i, j: (layer, 0, j)),
        ],
        out_specs=pl.BlockSpec((tm, tn), lambda i, j: (i, j)),
        out_shape=jax.ShapeDtypeStruct((t, n), out_dtype),
        scratch_shapes=[pltpu.VMEM((tm, d), BF16)],
        compiler_params=_params("parallel", "arbitrary"),
        name="in_proj",
    )(x, g, w)


def _dt_prep_kernel(x_ref, g_ref, w_ref, bias_ref, a_ref, dt_ref, ac_ref, act_ref):
    hn = _rms(x_ref[...], g_ref[...]).astype(BF16)
    raw = jnp.dot(hn, w_ref[...], preferred_element_type=F32) + bias_ref[...]
    dt = jnp.maximum(raw, 0.0) + jnp.log1p(jnp.exp(-jnp.abs(raw)))
    a = dt * a_ref[...]
    s = a.shape[0]
    pos = lax.broadcasted_iota(jnp.int32, a.shape, 0) & (SSM_CHUNK - 1)
    pre = a
    suf = a
    k = 1
    while k < SSM_CHUNK:
        pre = pre + jnp.where(pos >= k, pltpu.roll(pre, k, 0), 0.0)
        suf = suf + jnp.where(pos < SSM_CHUNK - k, pltpu.roll(suf, s - k, 0), 0.0)
        k *= 2
    lane = lax.broadcasted_iota(jnp.int32, a.shape, 1)
    is_fwd = (lane & (DT_LANES_PER_GROUP - 1)) < SSM_HEADS_PER_GROUP
    ac = jnp.where(is_fwd, pre, suf)
    act_ref[...] = ac.T
    for grp in range(SSM_GROUPS):
        shift = (LANES - grp * DT_LANES_PER_GROUP) % LANES
        dt_ref[:, grp * LANES:(grp + 1) * LANES] = dt if shift == 0 else pltpu.roll(dt, shift, 1)
        ac_ref[:, grp * LANES:(grp + 1) * LANES] = ac if shift == 0 else pltpu.roll(ac, shift, 1)


def _dt_prep(x3, g, w_dt, bias, a_neg):
    b, s, d = x3.shape
    out = jax.ShapeDtypeStruct((b, s, SSM_GROUPS * LANES), F32)
    out_t = jax.ShapeDtypeStruct((b, LANES, s), F32)
    vec = pl.BlockSpec((1, LANES), lambda i: (0, 0))
    return pl.pallas_call(
        _dt_prep_kernel,
        grid=(b,),
        in_specs=[
            pl.BlockSpec((None, s, d), lambda i: (i, 0, 0)),
            pl.BlockSpec((1, d), lambda i: (0, 0)),
            pl.BlockSpec((d, LANES), lambda i: (0, 0)),
            vec,
            vec,
        ],
        out_specs=[
            pl.BlockSpec((None, s, SSM_GROUPS * LANES), lambda i: (i, 0, 0)),
            pl.BlockSpec((None, s, SSM_GROUPS * LANES), lambda i: (i, 0, 0)),
            pl.BlockSpec((None, LANES, s), lambda i: (i, 0, 0)),
        ],
        out_shape=[out, out, out_t],
        compiler_params=_params("parallel"),
        name="dt_prep",
    )(x3, g, w_dt, bias, a_neg)


def _conv_kernel(x_ref, w_ref, b_ref, o_ref):
    xf = x_ref[...].astype(F32)
    s = xf.shape[0]
    row = lax.broadcasted_iota(jnp.int32, xf.shape, 0)
    pad = SSM_CONV // 2
    acc = xf * w_ref[pad:pad + 1, :] + b_ref[...]
    for j in range(SSM_CONV):
        shift = pad - j
        if shift == 0:
            continue
        rolled = pltpu.roll(xf, shift % s, 0)
        valid = (row >= shift) & (row < s + shift)
        acc = acc + jnp.where(valid, rolled, 0.0) * w_ref[j:j + 1, :]
    o_ref[...] = (acc * jax.nn.sigmoid(acc)).astype(o_ref.dtype)


def _conv_silu(proj3, conv_w, conv_b, tc):
    b, s, _ = proj3.shape
    col0 = COL_XBC // tc
    return pl.pallas_call(
        _conv_kernel,
        grid=(b, SSM_CONV_CH // tc),
        in_specs=[
            pl.BlockSpec((None, s, tc), lambda i, c: (i, 0, col0 + c)),
            pl.BlockSpec((SSM_CONV, tc), lambda i, c: (0, c)),
            pl.BlockSpec((1, tc), lambda i, c: (0, c)),
        ],
        out_specs=pl.BlockSpec((None, s, tc), lambda i, c: (i, 0, c)),
        out_shape=jax.ShapeDtypeStruct((b, s, SSM_CONV_CH), BF16),
        compiler_params=_params("parallel", "parallel"),
        name="conv_silu",
    )(proj3, conv_w, conv_b)


def _rel_bucket(rel):
    half = NUM_BUCKETS // 2
    max_exact = half // 2
    ret = jnp.where(rel > 0, half, 0)
    n = jnp.abs(rel)
    n_safe = jnp.maximum(n, 1).astype(F32)
    large = max_exact + (jnp.log(n_safe / max_exact) / math.log(MAX_DISTANCE / max_exact)
                         * (half - max_exact)).astype(jnp.int32)
    large = jnp.minimum(large, half - 1)
    return ret + jnp.where(n < max_exact, n, large)


def _bias_kernel(relb_ref, bid_ref, o_ref, *, tq):
    h = pl.program_id(0)
    bid = bid_ref[...]
    fv = jnp.zeros(bid.shape, F32)
    for bkt in range(NUM_BUCKETS):
        fv = jnp.where(bid == bkt, relb_ref[h, bkt], fv)
    g = jnp.broadcast_to(fv, o_ref.shape)
    row = lax.broadcasted_iota(jnp.int32, o_ref.shape, 0)
    k = 1
    while k < tq:
        g = jnp.where((row & k) != 0, pltpu.roll(g, k, 1), g)
        k *= 2
    o_ref[...] = g


def _bias_windows(rel_bias, s, tq):
    width = 2 * s
    m = jnp.arange(width, dtype=jnp.int32)
    bid = _rel_bucket(((m + tq) % width) - s).astype(jnp.int32)[None, :]
    return pl.pallas_call(
        functools.partial(_bias_kernel, tq=tq),
        grid=(ATTN_HEADS,),
        in_specs=[
            pl.BlockSpec(memory_space=pltpu.SMEM),
            pl.BlockSpec((1, width), lambda h: (0, 0)),
        ],
        out_specs=pl.BlockSpec((None, tq, width), lambda h: (h, 0, 0)),
        out_shape=jax.ShapeDtypeStruct((ATTN_HEADS, tq, width), F32),
        compiler_params=_params("parallel"),
        name="rel_bias_windows",
    )(rel_bias.T.astype(F32), bid)


SOFTMAX_SHIFT_SLACK = 40.0


def _attn_kernel(relb_ref, q_ref, k_ref, v_ref, bias_ref, gqk_ref, lamqk_ref, gh_ref, o_ref, q0_ref, q1_ref, kn_ref,
                 vx_ref, shift_ref, slack_ref, *, tq, lam_init):
    h = pl.program_id(0)
    qi = pl.program_id(2)
    s = k_ref.shape[0]
    lo = lax.broadcasted_iota(jnp.int32, (1, LANES), 1) < ATTN_HEAD_DIM

    def half_sums(x2):
        s_all = jnp.sum(x2, axis=-1, keepdims=True)
        s_lo = jnp.sum(jnp.where(lo, x2, 0.0), axis=-1, keepdims=True)
        return s_lo, jnp.maximum(s_all - s_lo, 0.0)

    def half_norm(x, g):
        s_lo, s_hi = half_sums(x * x)
        ms = jnp.where(lo, s_lo, s_hi) * (1.0 / ATTN_HEAD_DIM)
        return x * lax.rsqrt(ms + RMS_EPS) * g

    @pl.when(qi == 0)
    def _():
        kn = half_norm(k_ref[...].astype(F32), gqk_ref[1:2, :]).astype(BF16)
        kn_ref[...] = kn
        k_lo, k_hi = half_sums(jnp.square(kn.astype(F32)))
        kmax = jnp.where(lo, jnp.sqrt(jnp.max(k_lo, axis=0, keepdims=True)),
                         jnp.sqrt(jnp.max(k_hi, axis=0, keepdims=True)))
        vx_ref[:, :LANES] = v_ref[...]
        vx_ref[:, LANES:] = jnp.ones((s, LANES), BF16)
        bmax = relb_ref[h, 0]
        bmin = relb_ref[h, 0]
        for bkt in range(1, NUM_BUCKETS):
            bmax = jnp.maximum(bmax, relb_ref[h, bkt])
            bmin = jnp.minimum(bmin, relb_ref[h, bkt])
        qn = (half_norm(q_ref[...].astype(F32), gqk_ref[0:1, :]) * (ATTN_HEAD_DIM ** -0.5)).astype(BF16)
        q0_ref[...] = jnp.where(lo, qn, jnp.zeros_like(qn))
        q1_ref[...] = jnp.where(lo, jnp.zeros_like(qn), qn)
        q_lo, q_hi = half_sums(jnp.square(qn.astype(F32)))
        reach = jnp.sqrt(jnp.where(lo, q_lo, q_hi)) * kmax
        shift_ref[...] = reach + bmax
        slack_ref[0] = 2.0 * jnp.max(reach) + (bmax - bmin)

    rows = pl.ds(pl.multiple_of(qi * tq, tq), tq)
    q0 = q0_ref[rows, :]
    q1 = q1_ref[rows, :]
    slack = slack_ref[0]
    c0 = pl.multiple_of(s - (qi + 1) * tq, tq)
    lq = lamqk_ref[...]
    lam = (jnp.exp(jnp.sum(lq[0:1] * lq[1:2], axis=-1, keepdims=True))
           - jnp.exp(jnp.sum(lq[2:3] * lq[3:4], axis=-1, keepdims=True)) + lam_init)

    def logits_of(qm):
        return lax.dot_general(qm, kn_ref[...], (((1,), (1,)), ((), ())),
                               preferred_element_type=F32) + bias_ref[:, pl.ds(c0, s)]

    def finish(out):
        o_ref[...] = (_rms(out, gh_ref[...]) * (1.0 - lam_init)).astype(o_ref.dtype)

    @pl.when(slack <= SOFTMAX_SHIFT_SLACK)
    def _():
        shift = shift_ref[rows, :]

        def weighted_v(qm, m):
            p = jnp.exp(logits_of(qm) - m).astype(BF16)
            return jnp.dot(p, vx_ref[...], preferred_element_type=F32)

        o0 = weighted_v(q0, shift[:, 0:1])
        o1 = weighted_v(q1, shift[:, ATTN_HEAD_DIM:ATTN_HEAD_DIM + 1])
        finish(o0[:, :LANES] / o0[:, LANES:LANES + 1] - o1[:, :LANES] * (lam / o1[:, LANES:LANES + 1]))

    @pl.when(slack > SOFTMAX_SHIFT_SLACK)
    def _():
        def softmax_map(qm):
            logits = logits_of(qm)
            p = jnp.exp(logits - jnp.max(logits, axis=-1, keepdims=True))
            return p, jnp.sum(p, axis=-1, keepdims=True)

        p0, l0 = softmax_map(q0)
        p1, l1 = softmax_map(q1)
        attn = p0 * (1.0 / l0) - p1 * (lam / l1)
        finish(jnp.dot(attn.astype(BF16), v_ref[...], preferred_element_type=F32))


def _diff_attention(proj3, bias_win, relb_t, gqk, lamqk, gh, tq, lam_init):
    b, s, _ = proj3.shape
    width = bias_win.shape[-1]
    return pl.pallas_call(
        functools.partial(_attn_kernel, tq=tq, lam_init=lam_init),
        grid=(ATTN_HEADS, b, s // tq),
        in_specs=[
            pl.BlockSpec(memory_space=pltpu.SMEM),
            pl.BlockSpec((None, s, LANES), lambda h, i, q: (i, 0, COL_Q // LANES + h)),
            pl.BlockSpec((None, s, LANES), lambda h, i, q: (i, 0, COL_K // LANES + h)),
            pl.BlockSpec((None, s, LANES), lambda h, i, q: (i, 0, COL_V // LANES + h)),
            pl.BlockSpec((None, tq, width), lambda h, i, q: (h, 0, 0)),
            pl.BlockSpec((2, LANES), lambda h, i, q: (0, 0)),
            pl.BlockSpec((4, ATTN_HEAD_DIM), lambda h, i, q: (0, 0)),
            pl.BlockSpec((1, LANES), lambda h, i, q: (0, 0)),
        ],
        out_specs=pl.BlockSpec((None, tq, LANES), lambda h, i, q: (i, q, h)),
        out_shape=jax.ShapeDtypeStruct((b, s, ATTN_WIDTH), BF16),
        scratch_shapes=[pltpu.VMEM((s, LANES), BF16), pltpu.VMEM((s, LANES), BF16), pltpu.VMEM((s, LANES), BF16),
                        pltpu.VMEM((s, 2 * LANES), BF16), pltpu.VMEM((s, LANES), F32), pltpu.SMEM((1,), F32)],
        compiler_params=_params("parallel", "parallel", "arbitrary"),
        name="diff_attention",
    )(relb_t, proj3, proj3, proj3, bias_win, gqk, lamqk, gh)


def _ssd_kernel(x_ref, b_ref, c_ref, dt_ref, ac_ref, act_ref, dsk_ref, y_ref, st_ref, acc_ref):
    L = SSM_CHUNK
    s = x_ref.shape[0]
    n_chunks = s // L
    row = lax.broadcasted_iota(jnp.int32, (L, L), 0)
    col = lax.broadcasted_iota(jnp.int32, (L, L), 1)
    lo = lax.broadcasted_iota(jnp.int32, (L, LANES), 1) < SSM_HEAD_DIM
    lo1 = lo[0:1, :]
    n_pairs = SSM_HEADS_PER_GROUP // 2

    def one_chunk(c, backward):
        r0 = pl.multiple_of(c * L, L)
        off = SSM_HEADS_PER_GROUP if backward else 0
        tot_row = 0 if backward else L - 1
        mask = (row <= col) if backward else (row >= col)
        xc = x_ref[pl.ds(r0, L), :].astype(F32)
        bc = b_ref[pl.ds(r0, L), :]
        cc = c_ref[pl.ds(r0, L), :]
        dtc = dt_ref[pl.ds(r0, L), :]
        acc = ac_ref[pl.ds(r0, L), :]
        act = act_ref[:, pl.ds(r0, L)]
        cb = lax.dot_general(cc, bc, (((1,), (1,)), ((), ())), preferred_element_type=F32)
        y_off = jnp.dot(cc, st_ref[...].astype(BF16), preferred_element_type=F32)

        def col_of(arr, e):
            return jnp.broadcast_to(arr[:, off + e:off + e + 1], (L, LANES))

        ys, xdecs, tots = [], [], []
        for j in range(n_pairs):
            e0, e1 = 2 * j, 2 * j + 1
            dt_pair = jnp.where(lo, col_of(dtc, e0), col_of(dtc, e1))
            ac_pair = jnp.where(lo, col_of(acc, e0), col_of(acc, e1))
            tot_pair = jnp.where(lo1, acc[tot_row:tot_row + 1, off + e0:off + e0 + 1],
                                 acc[tot_row:tot_row + 1, off + e1:off + e1 + 1])
            xdt = xc[:, j * LANES:(j + 1) * LANES] * dt_pair
            xdt_b = xdt.astype(BF16)
            diag = []
            for e in (e0, e1):
                decay = jnp.where(mask, jnp.exp(col_of(acc, e) - act[off + e:off + e + 1, :]), 0.0)
                diag.append(jnp.dot((cb * decay).astype(BF16), xdt_b, preferred_element_type=F32))
            ys.append(jnp.where(lo, diag[0], diag[1]) + y_off[:, j * LANES:(j + 1) * LANES] * jnp.exp(ac_pair))
            xdecs.append((xdt * jnp.exp(tot_pair - ac_pair)).astype(BF16))
            tots.append(jnp.exp(tot_pair))
        y = jnp.concatenate(ys, axis=1)
        xdec = jnp.concatenate(xdecs, axis=1)
        st_new = lax.dot_general(bc, xdec, (((0,), (0,)), ((), ())), preferred_element_type=F32)
        st_ref[...] = st_ref[...] * jnp.concatenate(tots, axis=1) + st_new
        if backward:
            y_ref[pl.ds(r0, L), :] = (acc_ref[pl.ds(r0, L), :] + y).astype(y_ref.dtype)
        else:
            acc_ref[pl.ds(r0, L), :] = y + xc * dsk_ref[...]

    st_ref[...] = jnp.zeros_like(st_ref)

    def fwd_body(c, carry):
        one_chunk(c, False)
        return carry

    lax.fori_loop(0, n_chunks, fwd_body, 0)
    st_ref[...] = jnp.zeros_like(st_ref)

    def bwd_body(c, carry):
        one_chunk(n_chunks - 1 - c, True)
        return carry

    lax.fori_loop(0, n_chunks, bwd_body, 0)


def _ssd_scan(u, dt_g, ac_g, act_g, dskip):
    b, s, _ = u.shape
    gw = SSM_HEADS_PER_GROUP * SSM_HEAD_DIM
    xcols = SSM_INNER // SSM_STATE
    nl = DT_LANES_PER_GROUP
    return pl.pallas_call(
        _ssd_kernel,
        grid=(b, SSM_GROUPS),
        in_specs=[
            pl.BlockSpec((None, s, gw), lambda i, g: (i, 0, g)),
            pl.BlockSpec((None, s, SSM_STATE), lambda i, g: (i, 0, xcols + g)),
            pl.BlockSpec((None, s, SSM_STATE), lambda i, g: (i, 0, xcols + SSM_GROUPS + g)),
            pl.BlockSpec((None, s, LANES), lambda i, g: (i, 0, g)),
            pl.BlockSpec((None, s, LANES), lambda i, g: (i, 0, g)),
            pl.BlockSpec((None, nl, s), lambda i, g: (i, g, 0)),
            pl.BlockSpec((1, gw), lambda i, g: (0, g)),
        ],
        out_specs=pl.BlockSpec((None, s, gw), lambda i, g: (i, 0, g)),
        out_shape=jax.ShapeDtypeStruct((b, s, SSM_INNER), BF16),
        scratch_shapes=[pltpu.VMEM((SSM_STATE, gw), F32), pltpu.VMEM((s, gw), F32)],
        compiler_params=_params("parallel", "parallel"),
        name="ssd_scan",
    )(u, u, u, dt_g, ac_g, act_g, dskip)


def _route(logits):
    lane = lax.broadcasted_iota(jnp.int32, logits.shape, 1)
    lane_f = lane.astype(F32)
    neg = -jnp.inf
    big = float(LANES)
    is_g = lane < MOE_GROUPS
    gl = jnp.where(is_g, logits, neg)
    gmax = jnp.max(gl, axis=-1, keepdims=True)
    g_idx = jnp.min(jnp.where(gl == gmax, lane_f, big), axis=-1, keepdims=True)
    g_gate = 1.0 / jnp.sum(jnp.where(is_g, jnp.exp(logits - gmax), 0.0), axis=-1, keepdims=True)
    first = MOE_GROUPS + MOE_EXPERTS_PER_GROUP * g_idx
    in_grp = (lane_f >= first) & (lane_f < first + MOE_EXPERTS_PER_GROUP)
    el = jnp.where(in_grp, logits, neg)
    v1 = jnp.max(el, axis=-1, keepdims=True)
    i1 = jnp.min(jnp.where(el == v1, lane_f, big), axis=-1, keepdims=True)
    el2 = jnp.where(lane_f == i1, neg, el)
    v2 = jnp.max(el2, axis=-1, keepdims=True)
    i2 = jnp.min(jnp.where(el2 == v2, lane_f, big), axis=-1, keepdims=True)
    r = jnp.exp(v2 - v1)
    p1 = 1.0 / (1.0 + r)
    p2 = r * p1
    ids = jnp.where(lane == 0, i1 - MOE_GROUPS, jnp.where(lane == 1, i2 - MOE_GROUPS, 0.0)).astype(jnp.int32)
    gates = jnp.where(lane == 0, p1 * g_gate, jnp.where(lane == 1, p2 * g_gate, 0.0))
    return ids, gates


_HI_HALF = 0xFFFF0000


def _pack_bf16_pairs(x):
    n = x.shape[1] // 2
    bits = lax.bitcast_convert_type(x.astype(BF16).astype(F32), jnp.uint32)
    return (bits[:, n:] & jnp.uint32(_HI_HALF)) | (bits[:, :n] >> 16)


def _unpack_bf16_pairs(w):
    lo = lax.bitcast_convert_type(w << 16, F32)
    hi = lax.bitcast_convert_type(w & jnp.uint32(_HI_HALF), F32)
    return jnp.concatenate([lo, hi], axis=1)


def _post_kernel(attn_ref, y_ref, z0_ref, z1_ref, ga_ref, gm_ref, x_ref, gs_ref, wa_ref, ws_ref, wo_ref, g2_ref,
                 wr_ref, br_ref, xo_ref, hn_ref, id_ref, gate_ref):
    ya = jnp.dot(attn_ref[...], wa_ref[...], preferred_element_type=F32)
    z = jnp.concatenate([z0_ref[...], z1_ref[...]], axis=1).astype(F32)
    gated = y_ref[...].astype(F32) * (z * jax.nn.sigmoid(z))
    ym = jnp.dot(_rms(gated, gs_ref[...]).astype(BF16), ws_ref[...], preferred_element_type=F32)
    merged = (jax.nn.sigmoid(ga_ref[...].astype(F32)) * ya + jax.nn.sigmoid(gm_ref[...].astype(F32)) * ym)
    x_new = x_ref[...] + jnp.dot(merged.astype(BF16), wo_ref[...], preferred_element_type=F32)
    xo_ref[...] = x_new
    hn = _rms(x_new, g2_ref[...])
    hn_ref[:, 0, :] = _pack_bf16_pairs(hn)
    hn_hi = hn.astype(BF16)
    hn_lo = (hn - hn_hi.astype(F32)).astype(BF16)
    wr = wr_ref[...]
    wr_hi = wr.astype(BF16)
    wr_lo = (wr - wr_hi.astype(F32)).astype(BF16)
    logits = (jnp.dot(hn_hi, wr_hi, preferred_element_type=F32) + jnp.dot(hn_lo, wr_hi, preferred_element_type=F32)
              + jnp.dot(hn_hi, wr_lo, preferred_element_type=F32) + br_ref[...])
    ids, gates = _route(logits)
    id_ref[...] = ids
    gate_ref[...] = gates


def _post_mixer(attn, y, proj, gate_proj, x, gs, wa, ws, wo, g2, wr, br, tm):
    t, d = x.shape
    row = lambda w, c: pl.BlockSpec((tm, w), lambda i: (i, c))
    full = lambda a: pl.BlockSpec(a.shape, lambda i: (0,) * a.ndim)
    return pl.pallas_call(
        _post_kernel,
        grid=(t // tm,),
        in_specs=[
            row(ATTN_WIDTH, 0),
            row(SSM_INNER, 0),
            row(D_MODEL, COL_Z // D_MODEL),
            row(D_MODEL, COL_Z // D_MODEL + 1),
            row(D_MODEL, 0),
            row(D_MODEL, 1),
            row(D_MODEL, 0),
            full(gs), full(wa), full(ws), full(wo), full(g2), full(wr), full(br),
        ],
        out_specs=[row(D_MODEL, 0), pl.BlockSpec((tm, 1, PACKED), lambda i: (i, 0, 0)), row(LANES, 0), row(LANES, 0)],
        out_shape=[
            jax.ShapeDtypeStruct((t, d), F32),
            jax.ShapeDtypeStruct((t, 1, PACKED), jnp.uint32),
            jax.ShapeDtypeStruct((t, LANES), jnp.int32),
            jax.ShapeDtypeStruct((t, LANES), F32),
        ],
        compiler_params=_params("parallel"),
        name="post_mixer",
    )(attn, y, proj, proj, gate_proj, gate_proj, x, gs, wa, ws, wo, g2, wr, br)


GATHER_DEPTH = 3


def _start_rows(idx_ref, base, src_hbm, dst_ref, sem):
    for r in range(dst_ref.shape[0]):
        pltpu.make_async_copy(src_hbm.at[idx_ref[base + r]], dst_ref.at[pl.ds(r, 1), :], sem).start()


def _wait_buffer(dst_ref, sem):
    pltpu.make_async_copy(dst_ref, dst_ref, sem).wait()


def _moe_kernel(blk_e_ref, pos_ref, nused_ref, hn_hbm, wg_ref, wu_ref, wd_ref, y_ref, src_ref, xg_ref, wgb_ref,
                wub_ref, wdb_ref, sem, *, n_tokens):
    i = pl.program_id(0)
    nused = nused_ref[0]

    def start(blk):
        slot = blk % GATHER_DEPTH
        _start_rows(src_ref, blk * MOE_BLOCK, hn_hbm, xg_ref.at[slot], sem.at[slot])

    @pl.when(i == 0)
    def _():
        def clear(j, carry):
            src_ref[j] = 0
            return carry

        lax.fori_loop(0, src_ref.shape[0], clear, 0, unroll=8)
        for k in range(MOE_TOPK):
            def fill(tok, carry):
                src_ref[pos_ref[k * n_tokens + tok]] = tok
                return carry

            lax.fori_loop(0, n_tokens, fill, 0, unroll=8)
        start(0)

    @pl.when((i == 0) & (nused > 1))
    def _():
        start(1)

    new_expert = (i == 0) | (blk_e_ref[i] != blk_e_ref[jnp.maximum(i - 1, 0)])

    @pl.when((i < nused) & new_expert)
    def _():
        wgb_ref[...] = wg_ref[...].astype(BF16)
        wub_ref[...] = wu_ref[...].astype(BF16)
        wdb_ref[...] = wd_ref[...].astype(BF16)

    def compute():
        xb = _unpack_bf16_pairs(xg_ref[i % GATHER_DEPTH]).astype(BF16)
        hg = jnp.dot(xb, wgb_ref[...], preferred_element_type=F32)
        hu = jnp.dot(xb, wub_ref[...], preferred_element_type=F32)
        hid = (hg * jax.nn.sigmoid(hg) * hu).astype(BF16)
        y_ref[:, 0, :] = _pack_bf16_pairs(jnp.dot(hid, wdb_ref[...], preferred_element_type=F32))

    def wait():
        slot = i % GATHER_DEPTH
        _wait_buffer(xg_ref.at[slot], sem.at[slot])

    @pl.when(i + 2 < nused)
    def _():
        wait()
        start(i + 2)
        compute()

    @pl.when((i < nused) & (i + 2 >= nused))
    def _():
        wait()
        compute()

    @pl.when(i >= nused)
    def _():
        y_ref[...] = jnp.zeros_like(y_ref)


def _moe_experts(blk_expert, pos_kt, nused, hn3, wg, wu, wd, layer):
    n_blocks = blk_expert.shape[0]
    d = wg.shape[2]
    n_tokens = hn3.shape[0]
    grid_spec = pltpu.PrefetchScalarGridSpec(
        num_scalar_prefetch=3,
        grid=(n_blocks,),
        in_specs=[
            pl.BlockSpec(memory_space=pl.ANY),
            pl.BlockSpec((None, None, d, MOE_FF), lambda i, be, st, nu: (layer, be[i], 0, 0)),
            pl.BlockSpec((None, None, d, MOE_FF), lambda i, be, st, nu: (layer, be[i], 0, 0)),
            pl.BlockSpec((None, None, MOE_FF, d), lambda i, be, st, nu: (layer, be[i], 0, 0)),
        ],
        out_specs=pl.BlockSpec((MOE_BLOCK, 1, PACKED), lambda i, be, st, nu: (i, 0, 0)),
        scratch_shapes=[
            pltpu.SMEM((n_blocks * MOE_BLOCK,), jnp.int32),
            pltpu.VMEM((GATHER_DEPTH, MOE_BLOCK, PACKED), jnp.uint32),
            pltpu.VMEM((d, MOE_FF), BF16),
            pltpu.VMEM((d, MOE_FF), BF16),
            pltpu.VMEM((MOE_FF, d), BF16),
            pltpu.SemaphoreType.DMA((GATHER_DEPTH,)),
        ],
    )
    return pl.pallas_call(
        functools.partial(_moe_kernel, n_tokens=n_tokens),
        grid_spec=grid_spec,
        out_shape=jax.ShapeDtypeStruct((n_blocks * MOE_BLOCK, 1, PACKED), jnp.uint32),
        compiler_params=_params("arbitrary"),
        name="moe_experts",
    )(blk_expert, pos_kt, nused, hn3, wg, wu, wd)


def _combine_kernel(pos_ref, x_ref, gate_ref, y_hbm, o_ref, yg_ref, sem, *, tt, n_tiles):
    i = pl.program_id(0)

    def start(tile):
        slot = tile % GATHER_DEPTH
        for k in range(MOE_TOPK):
            _start_rows(pos_ref, (k * n_tiles + tile) * tt, y_hbm, yg_ref.at[slot, k], sem.at[slot])

    def finish():
        gates = gate_ref[...]
        slot = i % GATHER_DEPTH
        o_ref[...] = (x_ref[...] + gates[:, 0:1] * _unpack_bf16_pairs(yg_ref[slot, 0])
                      + gates[:, 1:2] * _unpack_bf16_pairs(yg_ref[slot, 1]))

    def wait():
        slot = i % GATHER_DEPTH
        _wait_buffer(yg_ref.at[slot], sem.at[slot])

    @pl.when(i == 0)
    def _():
        start(0)
        if n_tiles > 1:
            start(1)

    @pl.when(i + 2 < n_tiles)
    def _():
        wait()
        start(i + 2)
        finish()

    @pl.when(i + 2 >= n_tiles)
    def _():
        wait()
        finish()


def _moe_combine(pos_kt, x, gates, y3, tt):
    t, d = x.shape
    n_tiles = t // tt
    grid_spec = pltpu.PrefetchScalarGridSpec(
        num_scalar_prefetch=1,
        grid=(n_tiles,),
        in_specs=[
            pl.BlockSpec((tt, d), lambda i, p: (i, 0)),
            pl.BlockSpec((tt, LANES), lambda i, p: (i, 0)),
            pl.BlockSpec(memory_space=pl.ANY),
        ],
        out_specs=pl.BlockSpec((tt, d), lambda i, p: (i, 0)),
        scratch_shapes=[pltpu.VMEM((GATHER_DEPTH, MOE_TOPK, tt, PACKED), jnp.uint32),
                        pltpu.SemaphoreType.DMA((GATHER_DEPTH,))],
    )
    return pl.pallas_call(
        functools.partial(_combine_kernel, tt=tt, n_tiles=n_tiles),
        grid_spec=grid_spec,
        out_shape=jax.ShapeDtypeStruct((t, d), F32),
        compiler_params=_params("arbitrary"),
        name="moe_combine",
    )(pos_kt, x, gates, y3)


def _plan_kernel(ids_ref, rank_ref, cnt_ref, carry_ref):
    @pl.when(pl.program_id(0) == 0)
    def _():
        carry_ref[...] = jnp.zeros_like(carry_ref)

    ids = ids_ref[...]
    tt = ids.shape[0]
    lane = lax.broadcasted_iota(jnp.int32, ids.shape, 1)
    oh0 = jnp.where(lane == ids[:, 0:1], 1.0, 0.0)
    oh1 = jnp.where(lane == ids[:, 1:2], 1.0, 0.0)
    earlier = jnp.where(lax.broadcasted_iota(jnp.int32, (tt, tt), 1) < lax.broadcasted_iota(jnp.int32, (tt, tt), 0),
                        1.0, 0.0).astype(BF16)
    before0 = jnp.dot(earlier, oh0.astype(BF16), preferred_element_type=F32)
    before1 = jnp.dot(earlier, oh1.astype(BF16), preferred_element_type=F32)
    tot0 = jnp.sum(oh0, axis=0, keepdims=True)
    carry = carry_ref[...]
    rank0 = jnp.sum(oh0 * (before0 + carry), axis=-1, keepdims=True)
    rank1 = jnp.sum(oh1 * (before1 + tot0 + carry), axis=-1, keepdims=True)
    carry = carry + tot0 + jnp.sum(oh1, axis=0, keepdims=True)
    carry_ref[...] = carry
    cnt_ref[...] = carry
    rank_ref[...] = jnp.where(lane == 0, rank0, jnp.where(lane == 1, rank1, 0.0)).astype(jnp.int32)


def _dispatch_plan(ids, tt):
    t = ids.shape[0]
    n_assign = t * MOE_TOPK
    rank, cnt = pl.pallas_call(
        _plan_kernel,
        grid=(t // tt,),
        in_specs=[pl.BlockSpec((tt, LANES), lambda i: (i, 0))],
        out_specs=[pl.BlockSpec((tt, LANES), lambda i: (i, 0)), pl.BlockSpec((1, LANES), lambda i: (0, 0))],
        out_shape=[jax.ShapeDtypeStruct((t, LANES), jnp.int32), jax.ShapeDtypeStruct((1, LANES), F32)],
        scratch_shapes=[pltpu.VMEM((1, LANES), F32)],
        compiler_params=_params("arbitrary"),
        name="moe_plan",
    )(ids)
    counts = cnt[0, :MOE_EXPERTS].astype(jnp.int32)
    padded = (counts + MOE_BLOCK - 1) // MOE_BLOCK * MOE_BLOCK
    pad_end = jnp.cumsum(padded)
    pad_start = pad_end - padded
    e_ids = ids[:, :MOE_TOPK]
    onehot = e_ids[:, :, None] == jnp.arange(MOE_EXPERTS, dtype=jnp.int32)[None, None, :]
    dest = jnp.sum(jnp.where(onehot, pad_start[None, None, :], 0), axis=-1) + rank[:, :MOE_TOPK]
    n_blocks = -(-n_assign // MOE_BLOCK) + MOE_EXPERTS
    blk_row0 = jnp.arange(n_blocks, dtype=jnp.int32) * MOE_BLOCK
    blk_expert = jnp.minimum(jnp.sum((pad_end[None, :] <= blk_row0[:, None]).astype(jnp.int32), axis=1),
                             MOE_EXPERTS - 1)
    nused = (pad_end[-1:] // MOE_BLOCK).astype(jnp.int32)
    pos_kt = dest.T.reshape(-1).astype(jnp.int32)
    return blk_expert, nused, pos_kt


def _pick(n, pref):
    return pref if n % pref == 0 else n


def kernel(x, norm1_g, w_in, qk_norm_g, lambda_qk, attn_head_norm_g, rel_bias, conv_w, conv_b, dt_bias, a_log,
           d_skip, ssm_norm_g, w_attn_out, w_ssm_out, w_out, norm2_g, w_router_group, b_router_group,
           w_router_expert, b_router_expert, w_exp_gate, w_exp_up, w_exp_down):
    b, s, d = x.shape
    t = b * s
    tq = _pick(s, 256)
    bias_win = _bias_windows(rel_bias, s, tq)

    lane = np.arange(DT_COLS)
    grp, rem = lane // DT_LANES_PER_GROUP, lane % DT_LANES_PER_GROUP
    direction, e = rem // SSM_HEADS_PER_GROUP, rem % SSM_HEADS_PER_GROUP
    dt_src = direction * SSM_HEADS + grp * SSM_HEADS_PER_GROUP + e
    pad_lanes = LANES - DT_COLS

    xf = x.reshape(t, d)
    for i in range(DEPTH):
        lam_init = 0.8 - 0.6 * math.exp(-0.3 * i)
        g1 = norm1_g[i][None, :]
        w_tail = lax.slice(w_in, (i, 0, MAIN_COLS), (i + 1, d, w_in.shape[2]))
        w_dt = jnp.pad(w_tail[0, :, :DT_COLS][:, dt_src], ((0, 0), (0, pad_lanes))).astype(BF16)
        dt_b = jnp.pad(dt_bias[i].reshape(-1)[dt_src], (0, pad_lanes))[None, :]
        a_neg = jnp.pad(-jnp.exp(a_log[i].astype(F32)).reshape(-1)[dt_src], (0, pad_lanes))[None, :]

        tm_proj = _pick(t, 1024)
        proj = _norm_matmul(xf, g1, w_in, i, MAIN_COLS, BF16, tm_proj, 1024)
        gate_proj = _norm_matmul(xf, g1, w_tail[:, :, DT_COLS:], 0, 2 * D_MODEL, BF16, tm_proj, 1024)
        proj3 = proj.reshape(b, s, MAIN_COLS)
        dt, ac, act = _dt_prep(xf.reshape(b, s, d), g1, w_dt, dt_b, a_neg)

        u = _conv_silu(proj3, conv_w[i], conv_b[i][None, :], 256)
        y_ssd = _ssd_scan(u, dt, ac, act, jnp.repeat(d_skip[i], SSM_HEAD_DIM)[None, :])

        gqk = jnp.tile(qk_norm_g[i], (1, 2))
        attn = _diff_attention(proj3, bias_win, rel_bias.T.astype(F32), gqk, lambda_qk[i],
                               attn_head_norm_g[i][None, :], tq, lam_init)

        w_r = jnp.pad(jnp.concatenate([w_router_group[i], w_router_expert[i]], axis=1),
                      ((0, 0), (0, LANES - MOE_GROUPS - MOE_EXPERTS)))
        b_r = jnp.pad(jnp.concatenate([b_router_group[i], b_router_expert[i]]),
                      (0, LANES - MOE_GROUPS - MOE_EXPERTS))[None, :]
        x_mid, hn, ids, gates = _post_mixer(
            attn.reshape(t, ATTN_WIDTH), y_ssd.reshape(t, SSM_INNER), proj, gate_proj, xf, ssm_norm_g[i][None, :],
            w_attn_out[i].astype(BF16), w_ssm_out[i].astype(BF16), w_out[i].astype(BF16), norm2_g[i][None, :],
            w_r, b_r, _pick(t, 256))

        blk_expert, nused, pos_kt = _dispatch_plan(ids, _pick(t, 256))
        y_pad = _moe_experts(blk_expert, pos_kt, nused, hn, w_exp_gate, w_exp_up, w_exp_down, i)
        xf = _moe_combine(pos_kt, x_mid, gates, y_pad, _pick(t, 128))
    return xf.reshape(b, s, d)
```

```python
import functools
import math

import jax
import jax.numpy as jnp
import numpy as np
from jax import lax
from jax.experimental import pallas as pl
from jax.experimental.pallas import tpu as pltpu

D_MODEL = 1024
DEPTH = 2
ATTN_HEADS = 8
ATTN_HEAD_DIM = 64
ATTN_V_DIM = 2 * ATTN_HEAD_DIM
ATTN_WIDTH = ATTN_HEADS * ATTN_V_DIM
NUM_BUCKETS = 32
MAX_DISTANCE = 128
SSM_INNER = 2 * D_MODEL
SSM_HEAD_DIM = 64
SSM_HEADS = SSM_INNER // SSM_HEAD_DIM
SSM_GROUPS = 4
SSM_HEADS_PER_GROUP = SSM_HEADS // SSM_GROUPS
SSM_STATE = 128
SSM_CONV = 5
SSM_CHUNK = 128
SSM_CONV_CH = SSM_INNER + 2 * SSM_GROUPS * SSM_STATE
MOE_GROUPS = 4
MOE_EXPERTS_PER_GROUP = 8
MOE_EXPERTS = MOE_GROUPS * MOE_EXPERTS_PER_GROUP
MOE_TOPK = 2
MOE_FF = 512
MOE_BLOCK = 128
RMS_EPS = 1e-6

LANES = 128
COL_Q = 0
COL_K = COL_Q + ATTN_WIDTH
COL_V = COL_K + ATTN_WIDTH
COL_Z = COL_V + ATTN_WIDTH
COL_XBC = COL_Z + SSM_INNER
MAIN_COLS = COL_XBC + SSM_CONV_CH
DT_COLS = 2 * SSM_HEADS
COL_GATES = MAIN_COLS + DT_COLS
COL_GA = MAIN_COLS
COL_GM = COL_GA + D_MODEL
PROJ_COLS = COL_GM + D_MODEL
DT_LANES_PER_GROUP = 2 * SSM_HEADS_PER_GROUP
PACKED = D_MODEL // 2

VMEM_LIMIT_BYTES = 56 * 1024 * 1024

F32 = jnp.float32
BF16 = jnp.bfloat16


def _params(*semantics):
    return pltpu.CompilerParams(dimension_semantics=semantics, vmem_limit_bytes=VMEM_LIMIT_BYTES)


def _rms(x, g):
    ms = jnp.mean(x * x, axis=-1, keepdims=True)
    return x * lax.rsqrt(ms + RMS_EPS) * g


def _norm_matmul_kernel(x_ref, g_ref, w_ref, wg_ref, o_ref, hn_ref, *, n_main):
    j = pl.program_id(1)

    @pl.when(j == 0)
    def _():
        hn_ref[...] = _rms(x_ref[...], g_ref[...]).astype(hn_ref.dtype)

    @pl.when(j < n_main)
    def _():
        o_ref[...] = jnp.dot(hn_ref[...], w_ref[...].astype(BF16), preferred_element_type=F32).astype(o_ref.dtype)

    @pl.when(j >= n_main)
    def _():
        o_ref[...] = jnp.dot(hn_ref[...], wg_ref[...].astype(BF16), preferred_element_type=F32).astype(o_ref.dtype)


def _norm_matmul(x, g, w, layer, w_gates, out_dtype, tm, tn):
    t, d = x.shape
    n_main = MAIN_COLS // tn
    n_gate = w_gates.shape[1] // tn
    return pl.pallas_call(
        functools.partial(_norm_matmul_kernel, n_main=n_main),
        grid=(t // tm, n_main + n_gate),
        in_specs=[
            pl.BlockSpec((tm, d), lambda i, j: (i, 0)),
            pl.BlockSpec((1, d), lambda i, j: (0, 0)),
            pl.BlockSpec((None, d, tn), lambda i, j: (layer, 0, jnp.minimum(j, n_main - 1))),
            pl.BlockSpec((d, tn), lambda i, j: (0, jnp.maximum(j - n_main, 0))),
        ],
        out_specs=pl.BlockSpec((tm, tn), lambda i, j: (i, j)),
        out_shape=jax.ShapeDtypeStruct((t, (n_main + n_gate) * tn), out_dtype),
        scratch_shapes=[pltpu.VMEM((tm, d), BF16)],
        compiler_params=_params("parallel", "arbitrary"),
        name="in_proj",
    )(x, g, w, w_gates)


def _dt_prep_kernel(x_ref, g_ref, w_ref, bias_ref, a_ref, dt_ref, ac_ref, act_ref):
    hn = _rms(x_ref[...], g_ref[...]).astype(BF16)
    raw = jnp.dot(hn, w_ref[...], preferred_element_type=F32) + bias_ref[...]
    dt = jnp.maximum(raw, 0.0) + jnp.log1p(jnp.exp(-jnp.abs(raw)))
    a = dt * a_ref[...]
    s = a.shape[0]
    pos = lax.broadcasted_iota(jnp.int32, a.shape, 0) & (SSM_CHUNK - 1)
    pre = a
    suf = a
    k = 1
    while k < SSM_CHUNK:
        pre = pre + jnp.where(pos >= k, pltpu.roll(pre, k, 0), 0.0)
        suf = suf + jnp.where(pos < SSM_CHUNK - k, pltpu.roll(suf, s - k, 0), 0.0)
        k *= 2
    lane = lax.broadcasted_iota(jnp.int32, a.shape, 1)
    is_fwd = (lane & (DT_LANES_PER_GROUP - 1)) < SSM_HEADS_PER_GROUP
    ac = jnp.where(is_fwd, pre, suf)
    act_ref[...] = ac.T
    for grp in range(SSM_GROUPS):
        shift = (LANES - grp * DT_LANES_PER_GROUP) % LANES
        dt_ref[:, grp * LANES:(grp + 1) * LANES] = dt if shift == 0 else pltpu.roll(dt, shift, 1)
        ac_ref[:, grp * LANES:(grp + 1) * LANES] = ac if shift == 0 else pltpu.roll(ac, shift, 1)


def _dt_prep(x3, g, w_dt, bias, a_neg):
    b, s, d = x3.shape
    out = jax.ShapeDtypeStruct((b, s, SSM_GROUPS * LANES), F32)
    out_t = jax.ShapeDtypeStruct((b, LANES, s), F32)
    vec = pl.BlockSpec((1, LANES), lambda i: (0, 0))
    return pl.pallas_call(
        _dt_prep_kernel,
        grid=(b,),
        in_specs=[
            pl.BlockSpec((None, s, d), lambda i: (i, 0, 0)),
            pl.BlockSpec((1, d), lambda i: (0, 0)),
            pl.BlockSpec((d, LANES), lambda i: (0, 0)),
            vec,
            vec,
        ],
        out_specs=[
            pl.BlockSpec((None, s, SSM_GROUPS * LANES), lambda i: (i, 0, 0)),
            pl.BlockSpec((None, s, SSM_GROUPS * LANES), lambda i: (i, 0, 0)),
            pl.BlockSpec((None, LANES, s), lambda i: (i, 0, 0)),
        ],
        out_shape=[out, out, out_t],
        compiler_params=_params("parallel"),
        name="dt_prep",
    )(x3, g, w_dt, bias, a_neg)


def _conv_kernel(x_ref, w_ref, b_ref, o_ref):
    xf = x_ref[...].astype(F32)
    s = xf.shape[0]
    row = lax.broadcasted_iota(jnp.int32, xf.shape, 0)
    pad = SSM_CONV // 2
    acc = xf * w_ref[pad:pad + 1, :] + b_ref[...]
    for j in range(SSM_CONV):
        shift = pad - j
        if shift == 0:
            continue
        rolled = pltpu.roll(xf, shift % s, 0)
        valid = (row >= shift) & (row < s + shift)
        acc = acc + jnp.where(valid, rolled, 0.0) * w_ref[j:j + 1, :]
    o_ref[...] = (acc * jax.nn.sigmoid(acc)).astype(o_ref.dtype)


def _conv_silu(proj3, conv_w, conv_b, tc):
    b, s, _ = proj3.shape
    col0 = COL_XBC // tc
    return pl.pallas_call(
        _conv_kernel,
        grid=(b, SSM_CONV_CH // tc),
        in_specs=[
            pl.BlockSpec((None, s, tc), lambda i, c: (i, 0, col0 + c)),
            pl.BlockSpec((SSM_CONV, tc), lambda i, c: (0, c)),
            pl.BlockSpec((1, tc), lambda i, c: (0, c)),
        ],
        out_specs=pl.BlockSpec((None, s, tc), lambda i, c: (i, 0, c)),
        out_shape=jax.ShapeDtypeStruct((b, s, SSM_CONV_CH), BF16),
        compiler_params=_params("parallel", "parallel"),
        name="conv_silu",
    )(proj3, conv_w, conv_b)


def _rel_bucket(rel):
    half = NUM_BUCKETS // 2
    max_exact = half // 2
    ret = jnp.where(rel > 0, half, 0)
    n = jnp.abs(rel)
    n_safe = jnp.maximum(n, 1).astype(F32)
    large = max_exact + (jnp.log(n_safe / max_exact) / math.log(MAX_DISTANCE / max_exact)
                         * (half - max_exact)).astype(jnp.int32)
    large = jnp.minimum(large, half - 1)
    return ret + jnp.where(n < max_exact, n, large)


def _bias_kernel(relb_ref, bid_ref, o_ref, *, tq):
    h = pl.program_id(0)
    bid = bid_ref[...]
    fv = jnp.zeros(bid.shape, F32)
    for bkt in range(NUM_BUCKETS):
        fv = jnp.where(bid == bkt, relb_ref[h, bkt], fv)
    g = jnp.broadcast_to(fv, o_ref.shape)
    row = lax.broadcasted_iota(jnp.int32, o_ref.shape, 0)
    k = 1
    while k < tq:
        g = jnp.where((row & k) != 0, pltpu.roll(g, k, 1), g)
        k *= 2
    o_ref[...] = g


def _bias_windows(rel_bias, s, tq):
    width = 2 * s
    m = jnp.arange(width, dtype=jnp.int32)
    bid = _rel_bucket(((m + tq) % width) - s).astype(jnp.int32)[None, :]
    return pl.pallas_call(
        functools.partial(_bias_kernel, tq=tq),
        grid=(ATTN_HEADS,),
        in_specs=[
            pl.BlockSpec(memory_space=pltpu.SMEM),
            pl.BlockSpec((1, width), lambda h: (0, 0)),
        ],
        out_specs=pl.BlockSpec((None, tq, width), lambda h: (h, 0, 0)),
        out_shape=jax.ShapeDtypeStruct((ATTN_HEADS, tq, width), F32),
        compiler_params=_params("parallel"),
        name="rel_bias_windows",
    )(rel_bias.T.astype(F32), bid)


SOFTMAX_SHIFT_SLACK = 40.0


def _attn_kernel(relb_ref, q_ref, k_ref, v_ref, bias_ref, gqk_ref, lamqk_ref, gh_ref, o_ref, q0_ref, q1_ref, kn_ref,
                 vx_ref, shift_ref, slack_ref, *, tq, lam_init):
    h = pl.program_id(0)
    qi = pl.program_id(2)
    s = k_ref.shape[0]
    lo = lax.broadcasted_iota(jnp.int32, (1, LANES), 1) < ATTN_HEAD_DIM

    def half_sums(x2):
        s_all = jnp.sum(x2, axis=-1, keepdims=True)
        s_lo = jnp.sum(jnp.where(lo, x2, 0.0), axis=-1, keepdims=True)
        return s_lo, jnp.maximum(s_all - s_lo, 0.0)

    def half_norm(x, g):
        s_lo, s_hi = half_sums(x * x)
        ms = jnp.where(lo, s_lo, s_hi) * (1.0 / ATTN_HEAD_DIM)
        return x * lax.rsqrt(ms + RMS_EPS) * g

    @pl.when(qi == 0)
    def _():
        kn = half_norm(k_ref[...].astype(F32), gqk_ref[1:2, :]).astype(BF16)
        kn_ref[...] = kn
        k_lo, k_hi = half_sums(jnp.square(kn.astype(F32)))
        kmax = jnp.where(lo, jnp.sqrt(jnp.max(k_lo, axis=0, keepdims=True)),
                         jnp.sqrt(jnp.max(k_hi, axis=0, keepdims=True)))
        vx_ref[:, :LANES] = v_ref[...]
        vx_ref[:, LANES:] = jnp.ones((s, LANES), BF16)
        bmax = relb_ref[h, 0]
        bmin = relb_ref[h, 0]
        for bkt in range(1, NUM_BUCKETS):
            bmax = jnp.maximum(bmax, relb_ref[h, bkt])
            bmin = jnp.minimum(bmin, relb_ref[h, bkt])
        qn = (half_norm(q_ref[...].astype(F32), gqk_ref[0:1, :]) * (ATTN_HEAD_DIM ** -0.5)).astype(BF16)
        q0_ref[...] = jnp.where(lo, qn, jnp.zeros_like(qn))
        q1_ref[...] = jnp.where(lo, jnp.zeros_like(qn), qn)
        q_lo, q_hi = half_sums(jnp.square(qn.astype(F32)))
        reach = jnp.sqrt(jnp.where(lo, q_lo, q_hi)) * kmax
        shift_ref[...] = reach + bmax
        slack_ref[0] = 2.0 * jnp.max(reach) + (bmax - bmin)

    rows = pl.ds(pl.multiple_of(qi * tq, tq), tq)
    q0 = q0_ref[rows, :]
    q1 = q1_ref[rows, :]
    slack = slack_ref[0]
    c0 = pl.multiple_of(s - (qi + 1) * tq, tq)
    lq = lamqk_ref[...]
    lam = (jnp.exp(jnp.sum(lq[0:1] * lq[1:2], axis=-1, keepdims=True))
           - jnp.exp(jnp.sum(lq[2:3] * lq[3:4], axis=-1, keepdims=True)) + lam_init)

    def logits_of(qm):
        return lax.dot_general(qm, kn_ref[...], (((1,), (1,)), ((), ())),
                               preferred_element_type=F32) + bias_ref[:, pl.ds(c0, s)]

    def finish(out):
        o_ref[...] = (_rms(out, gh_ref[...]) * (1.0 - lam_init)).astype(o_ref.dtype)

    @pl.when(slack <= SOFTMAX_SHIFT_SLACK)
    def _():
        shift = shift_ref[rows, :]

        def weighted_v(qm, m):
            p = jnp.exp(logits_of(qm) - m).astype(BF16)
            return jnp.dot(p, vx_ref[...], preferred_element_type=F32)

        o0 = weighted_v(q0, shift[:, 0:1])
        o1 = weighted_v(q1, shift[:, ATTN_HEAD_DIM:ATTN_HEAD_DIM + 1])
        finish(o0[:, :LANES] / o0[:, LANES:LANES + 1] - o1[:, :LANES] * (lam / o1[:, LANES:LANES + 1]))

    @pl.when(slack > SOFTMAX_SHIFT_SLACK)
    def _():
        def softmax_map(qm):
            logits = logits_of(qm)
            p = jnp.exp(logits - jnp.max(logits, axis=-1, keepdims=True))
            return p, jnp.sum(p, axis=-1, keepdims=True)

        p0, l0 = softmax_map(q0)
        p1, l1 = softmax_map(q1)
        attn = p0 * (1.0 / l0) - p1 * (lam / l1)
        finish(jnp.dot(attn.astype(BF16), v_ref[...], preferred_element_type=F32))


def _diff_attention(proj3, bias_win, relb_t, gqk, lamqk, gh, tq, lam_init):
    b, s, _ = proj3.shape
    width = bias_win.shape[-1]
    return pl.pallas_call(
        functools.partial(_attn_kernel, tq=tq, lam_init=lam_init),
        grid=(ATTN_HEADS, b, s // tq),
        in_specs=[
            pl.BlockSpec(memory_space=pltpu.SMEM),
            pl.BlockSpec((None, s, LANES), lambda h, i, q: (i, 0, COL_Q // LANES + h)),
            pl.BlockSpec((None, s, LANES), lambda h, i, q: (i, 0, COL_K // LANES + h)),
            pl.BlockSpec((None, s, LANES), lambda h, i, q: (i, 0, COL_V // LANES + h)),
            pl.BlockSpec((None, tq, width), lambda h, i, q: (h, 0, 0)),
            pl.BlockSpec((2, LANES), lambda h, i, q: (0, 0)),
            pl.BlockSpec((4, ATTN_HEAD_DIM), lambda h, i, q: (0, 0)),
            pl.BlockSpec((1, LANES), lambda h, i, q: (0, 0)),
        ],
        out_specs=pl.BlockSpec((None, tq, LANES), lambda h, i, q: (i, q, h)),
        out_shape=jax.ShapeDtypeStruct((b, s, ATTN_WIDTH), BF16),
        scratch_shapes=[pltpu.VMEM((s, LANES), BF16), pltpu.VMEM((s, LANES), BF16), pltpu.VMEM((s, LANES), BF16),
                        pltpu.VMEM((s, 2 * LANES), BF16), pltpu.VMEM((s, LANES), F32), pltpu.SMEM((1,), F32)],
        compiler_params=_params("parallel", "parallel", "arbitrary"),
        name="diff_attention",
    )(relb_t, proj3, proj3, proj3, bias_win, gqk, lamqk, gh)


def _ssd_kernel(x_ref, b_ref, c_ref, dt_ref, ac_ref, act_ref, dsk_ref, y_ref, st_ref, acc_ref):
    L = SSM_CHUNK
    s = x_ref.shape[0]
    n_chunks = s // L
    row = lax.broadcasted_iota(jnp.int32, (L, L), 0)
    col = lax.broadcasted_iota(jnp.int32, (L, L), 1)
    lo = lax.broadcasted_iota(jnp.int32, (L, LANES), 1) < SSM_HEAD_DIM
    lo1 = lo[0:1, :]
    n_pairs = SSM_HEADS_PER_GROUP // 2

    def one_chunk(c, backward):
        r0 = pl.multiple_of(c * L, L)
        off = SSM_HEADS_PER_GROUP if backward else 0
        tot_row = 0 if backward else L - 1
        mask = (row <= col) if backward else (row >= col)
        xc = x_ref[pl.ds(r0, L), :].astype(F32)
        bc = b_ref[pl.ds(r0, L), :]
        cc = c_ref[pl.ds(r0, L), :]
        dtc = dt_ref[pl.ds(r0, L), :]
        acc = ac_ref[pl.ds(r0, L), :]
        act = act_ref[:, pl.ds(r0, L)]
        cb = lax.dot_general(cc, bc, (((1,), (1,)), ((), ())), preferred_element_type=F32)
        y_off = jnp.dot(cc, st_ref[...].astype(BF16), preferred_element_type=F32)

        def col_of(arr, e):
            return jnp.broadcast_to(arr[:, off + e:off + e + 1], (L, LANES))

        ys, xdecs, tots = [], [], []
        for j in range(n_pairs):
            e0, e1 = 2 * j, 2 * j + 1
            dt_pair = jnp.where(lo, col_of(dtc, e0), col_of(dtc, e1))
            ac_pair = jnp.where(lo, col_of(acc, e0), col_of(acc, e1))
            tot_pair = jnp.where(lo1, acc[tot_row:tot_row + 1, off + e0:off + e0 + 1],
                                 acc[tot_row:tot_row + 1, off + e1:off + e1 + 1])
            xdt = xc[:, j * LANES:(j + 1) * LANES] * dt_pair
            xdt_b = xdt.astype(BF16)
            diag = []
            for e in (e0, e1):
                decay = jnp.where(mask, jnp.exp(col_of(acc, e) - act[off + e:off + e + 1, :]), 0.0)
                diag.append(jnp.dot((cb * decay).astype(BF16), xdt_b, preferred_element_type=F32))
            ys.append(jnp.where(lo, diag[0], diag[1]) + y_off[:, j * LANES:(j + 1) * LANES] * jnp.exp(ac_pair))
            xdecs.append((xdt * jnp.exp(tot_pair - ac_pair)).astype(BF16))
            tots.append(jnp.exp(tot_pair))
        y = jnp.concatenate(ys, axis=1)
        xdec = jnp.concatenate(xdecs, axis=1)
        st_new = lax.dot_general(bc, xdec, (((0,), (0,)), ((), ())), preferred_element_type=F32)
        st_ref[...] = st_ref[...] * jnp.concatenate(tots, axis=1) + st_new
        if backward:
            y_ref[pl.ds(r0, L), :] = (acc_ref[pl.ds(r0, L), :] + y).astype(y_ref.dtype)
        else:
            acc_ref[pl.ds(r0, L), :] = y + xc * dsk_ref[...]

    st_ref[...] = jnp.zeros_like(st_ref)

    def fwd_body(c, carry):
        one_chunk(c, False)
        return carry

    lax.fori_loop(0, n_chunks, fwd_body, 0)
    st_ref[...] = jnp.zeros_like(st_ref)

    def bwd_body(c, carry):
        one_chunk(n_chunks - 1 - c, True)
        return carry

    lax.fori_loop(0, n_chunks, bwd_body, 0)


def _ssd_scan(u, dt_g, ac_g, act_g, dskip):
    b, s, _ = u.shape
    gw = SSM_HEADS_PER_GROUP * SSM_HEAD_DIM
    xcols = SSM_INNER // SSM_STATE
    nl = DT_LANES_PER_GROUP
    return pl.pallas_call(
        _ssd_kernel,
        grid=(b, SSM_GROUPS),
        in_specs=[
            pl.BlockSpec((None, s, gw), lambda i, g: (i, 0, g)),
            pl.BlockSpec((None, s, SSM_STATE), lambda i, g: (i, 0, xcols + g)),
            pl.BlockSpec((None, s, SSM_STATE), lambda i, g: (i, 0, xcols + SSM_GROUPS + g)),
            pl.BlockSpec((None, s, LANES), lambda i, g: (i, 0, g)),
            pl.BlockSpec((None, s, LANES), lambda i, g: (i, 0, g)),
            pl.BlockSpec((None, nl, s), lambda i, g: (i, g, 0)),
            pl.BlockSpec((1, gw), lambda i, g: (0, g)),
        ],
        out_specs=pl.BlockSpec((None, s, gw), lambda i, g: (i, 0, g)),
        out_shape=jax.ShapeDtypeStruct((b, s, SSM_INNER), BF16),
        scratch_shapes=[pltpu.VMEM((SSM_STATE, gw), F32), pltpu.VMEM((s, gw), F32)],
        compiler_params=_params("parallel", "parallel"),
        name="ssd_scan",
    )(u, u, u, dt_g, ac_g, act_g, dskip)


def _route(logits):
    lane = lax.broadcasted_iota(jnp.int32, logits.shape, 1)
    lane_f = lane.astype(F32)
    neg = -jnp.inf
    big = float(LANES)
    is_g = lane < MOE_GROUPS
    gl = jnp.where(is_g, logits, neg)
    gmax = jnp.max(gl, axis=-1, keepdims=True)
    g_idx = jnp.min(jnp.where(gl == gmax, lane_f, big), axis=-1, keepdims=True)
    g_gate = 1.0 / jnp.sum(jnp.where(is_g, jnp.exp(logits - gmax), 0.0), axis=-1, keepdims=True)
    first = MOE_GROUPS + MOE_EXPERTS_PER_GROUP * g_idx
    in_grp = (lane_f >= first) & (lane_f < first + MOE_EXPERTS_PER_GROUP)
    el = jnp.where(in_grp, logits, neg)
    v1 = jnp.max(el, axis=-1, keepdims=True)
    i1 = jnp.min(jnp.where(el == v1, lane_f, big), axis=-1, keepdims=True)
    el2 = jnp.where(lane_f == i1, neg, el)
    v2 = jnp.max(el2, axis=-1, keepdims=True)
    i2 = jnp.min(jnp.where(el2 == v2, lane_f, big), axis=-1, keepdims=True)
    r = jnp.exp(v2 - v1)
    p1 = 1.0 / (1.0 + r)
    p2 = r * p1
    ids = jnp.where(lane == 0, i1 - MOE_GROUPS, jnp.where(lane == 1, i2 - MOE_GROUPS, 0.0)).astype(jnp.int32)
    gates = jnp.where(lane == 0, p1 * g_gate, jnp.where(lane == 1, p2 * g_gate, 0.0))
    return ids, gates


_HI_HALF = 0xFFFF0000


def _pack_bf16_pairs(x):
    n = x.shape[1] // 2
    bits = lax.bitcast_convert_type(x.astype(BF16).astype(F32), jnp.uint32)
    return (bits[:, n:] & jnp.uint32(_HI_HALF)) | (bits[:, :n] >> 16)


def _unpack_bf16_pairs(w):
    lo = lax.bitcast_convert_type(w << 16, F32)
    hi = lax.bitcast_convert_type(w & jnp.uint32(_HI_HALF), F32)
    return jnp.concatenate([lo, hi], axis=1)


def _post_kernel(attn_ref, y_ref, z0_ref, z1_ref, ga_ref, gm_ref, x_ref, gs_ref, wa_ref, ws_ref, wo_ref, g2_ref,
                 wr_ref, br_ref, xo_ref, hn_ref, id_ref, gate_ref):
    ya = jnp.dot(attn_ref[...], wa_ref[...], preferred_element_type=F32)
    z = jnp.concatenate([z0_ref[...], z1_ref[...]], axis=1).astype(F32)
    gated = y_ref[...].astype(F32) * (z * jax.nn.sigmoid(z))
    ym = jnp.dot(_rms(gated, gs_ref[...]).astype(BF16), ws_ref[...], preferred_element_type=F32)
    merged = (jax.nn.sigmoid(ga_ref[...].astype(F32)) * ya + jax.nn.sigmoid(gm_ref[...].astype(F32)) * ym)
    x_new = x_ref[...] + jnp.dot(merged.astype(BF16), wo_ref[...], preferred_element_type=F32)
    xo_ref[...] = x_new
    hn = _rms(x_new, g2_ref[...])
    hn_ref[:, 0, :] = _pack_bf16_pairs(hn)
    hn_hi = hn.astype(BF16)
    hn_lo = (hn - hn_hi.astype(F32)).astype(BF16)
    wr = wr_ref[...]
    wr_hi = wr.astype(BF16)
    wr_lo = (wr - wr_hi.astype(F32)).astype(BF16)
    logits = (jnp.dot(hn_hi, wr_hi, preferred_element_type=F32) + jnp.dot(hn_lo, wr_hi, preferred_element_type=F32)
              + jnp.dot(hn_hi, wr_lo, preferred_element_type=F32) + br_ref[...])
    ids, gates = _route(logits)
    id_ref[...] = ids
    gate_ref[...] = gates


def _post_mixer(attn, y, proj, x, gs, wa, ws, wo, g2, wr, br, tm):
    t, d = x.shape
    row = lambda w, c: pl.BlockSpec((tm, w), lambda i: (i, c))
    full = lambda a: pl.BlockSpec(a.shape, lambda i: (0,) * a.ndim)
    return pl.pallas_call(
        _post_kernel,
        grid=(t // tm,),
        in_specs=[
            row(ATTN_WIDTH, 0),
            row(SSM_INNER, 0),
            row(D_MODEL, COL_Z // D_MODEL),
            row(D_MODEL, COL_Z // D_MODEL + 1),
            row(D_MODEL, COL_GA // D_MODEL),
            row(D_MODEL, COL_GM // D_MODEL),
            row(D_MODEL, 0),
            full(gs), full(wa), full(ws), full(wo), full(g2), full(wr), full(br),
        ],
        out_specs=[row(D_MODEL, 0), pl.BlockSpec((tm, 1, PACKED), lambda i: (i, 0, 0)), row(LANES, 0), row(LANES, 0)],
        out_shape=[
            jax.ShapeDtypeStruct((t, d), F32),
            jax.ShapeDtypeStruct((t, 1, PACKED), jnp.uint32),
            jax.ShapeDtypeStruct((t, LANES), jnp.int32),
            jax.ShapeDtypeStruct((t, LANES), F32),
        ],
        compiler_params=_params("parallel"),
        name="post_mixer",
    )(attn, y, proj, proj, proj, proj, x, gs, wa, ws, wo, g2, wr, br)


GATHER_DEPTH = 3


def _start_rows(idx_ref, base, src_hbm, dst_ref, sem, rows=None):
    for r in (range(dst_ref.shape[0]) if rows is None else rows):
        pltpu.make_async_copy(src_hbm.at[idx_ref[base + r]], dst_ref.at[pl.ds(r, 1), :], sem).start(priority=r % 2)


def _wait_buffer(dst_ref, sem):
    pltpu.make_async_copy(dst_ref, dst_ref, sem).wait()


def _moe_kernel(blk_e_ref, pos_ref, nused_ref, seg_ref, hn_hbm, wg_ref, wu_ref, wd_ref, y_ref, src_ref, xg_ref,
                wgb_ref, wub_ref, wdb_ref, sem, *, n_tokens):
    i = pl.program_id(0)
    nused = nused_ref[0]
    quarter = MOE_BLOCK // 4

    def start(blk, part=None):
        slot = blk % GATHER_DEPTH
        rows = range(MOE_BLOCK) if part is None else range(part * quarter, (part + 1) * quarter)
        _start_rows(src_ref, blk * MOE_BLOCK, hn_hbm, xg_ref.at[slot], sem.at[slot], rows)

    @pl.when(i == 0)
    def _():
        def clear(j, carry):
            src_ref[j] = 0
            return carry

        for e in range(MOE_EXPERTS):
            lax.fori_loop(seg_ref[e], seg_ref[MOE_EXPERTS + e], clear, 0)
        for k in range(MOE_TOPK):
            def fill(tok, carry):
                src_ref[pos_ref[k * n_tokens + tok]] = tok
                return carry

            lax.fori_loop(0, n_tokens, fill, 0, unroll=8)
        start(0)

    @pl.when((i == 0) & (nused > 1))
    def _():
        start(1)

    new_expert = (i == 0) | (blk_e_ref[i] != blk_e_ref[jnp.maximum(i - 1, 0)])

    @pl.when((i < nused) & new_expert)
    def _():
        wgb_ref[...] = wg_ref[...].astype(BF16)
        wub_ref[...] = wu_ref[...].astype(BF16)
        wdb_ref[...] = wd_ref[...].astype(BF16)

    def compute(prefetch):
        xb = _unpack_bf16_pairs(xg_ref[i % GATHER_DEPTH]).astype(BF16)
        prefetch(0)
        hg = jnp.dot(xb, wgb_ref[...], preferred_element_type=F32)
        prefetch(1)
        hu = jnp.dot(xb, wub_ref[...], preferred_element_type=F32)
        prefetch(2)
        hid = (hg * jax.nn.sigmoid(hg) * hu).astype(BF16)
        y = jnp.dot(hid, wdb_ref[...], preferred_element_type=F32)
        prefetch(3)
        y_ref[:, 0, :] = _pack_bf16_pairs(y)

    def wait():
        slot = i % GATHER_DEPTH
        _wait_buffer(xg_ref.at[slot], sem.at[slot])

    @pl.when(i + 2 < nused)
    def _():
        wait()
        compute(lambda part: start(i + 2, part))

    @pl.when((i < nused) & (i + 2 >= nused))
    def _():
        wait()
        compute(lambda part: None)

    @pl.when(i >= nused)
    def _():
        y_ref[...] = jnp.zeros_like(y_ref)


def _moe_experts(blk_expert, pos_kt, nused, seg, hn3, wg, wu, wd, layer):
    n_blocks = blk_expert.shape[0]
    d = wg.shape[2]
    n_tokens = hn3.shape[0]
    grid_spec = pltpu.PrefetchScalarGridSpec(
        num_scalar_prefetch=4,
        grid=(n_blocks,),
        in_specs=[
            pl.BlockSpec(memory_space=pl.ANY),
            pl.BlockSpec((None, None, d, MOE_FF), lambda i, be, st, nu, sg: (layer, be[i], 0, 0)),
            pl.BlockSpec((None, None, d, MOE_FF), lambda i, be, st, nu, sg: (layer, be[i], 0, 0)),
            pl.BlockSpec((None, None, MOE_FF, d), lambda i, be, st, nu, sg: (layer, be[i], 0, 0)),
        ],
        out_specs=pl.BlockSpec((MOE_BLOCK, 1, PACKED), lambda i, be, st, nu, sg: (i, 0, 0)),
        scratch_shapes=[
            pltpu.SMEM((n_blocks * MOE_BLOCK,), jnp.int32),
            pltpu.VMEM((GATHER_DEPTH, MOE_BLOCK, PACKED), jnp.uint32),
            pltpu.VMEM((d, MOE_FF), BF16),
            pltpu.VMEM((d, MOE_FF), BF16),
            pltpu.VMEM((MOE_FF, d), BF16),
            pltpu.SemaphoreType.DMA((GATHER_DEPTH,)),
        ],
    )
    return pl.pallas_call(
        functools.partial(_moe_kernel, n_tokens=n_tokens),
        grid_spec=grid_spec,
        out_shape=jax.ShapeDtypeStruct((n_blocks * MOE_BLOCK, 1, PACKED), jnp.uint32),
        compiler_params=_params("arbitrary"),
        name="moe_experts",
    )(blk_expert, pos_kt, nused, seg, hn3, wg, wu, wd)


def _combine_kernel(pos_ref, x_ref, gate_ref, y_hbm, o_ref, yg_ref, sem, *, tt, n_tiles):
    i = pl.program_id(0)

    def start(tile):
        slot = tile % GATHER_DEPTH
        for k in range(MOE_TOPK):
            _start_rows(pos_ref, (k * n_tiles + tile) * tt, y_hbm, yg_ref.at[slot, k], sem.at[slot])

    def finish():
        gates = gate_ref[...]
        slot = i % GATHER_DEPTH
        o_ref[...] = (x_ref[...] + gates[:, 0:1] * _unpack_bf16_pairs(yg_ref[slot, 0])
                      + gates[:, 1:2] * _unpack_bf16_pairs(yg_ref[slot, 1]))

    def wait():
        slot = i % GATHER_DEPTH
        _wait_buffer(yg_ref.at[slot], sem.at[slot])

    @pl.when(i == 0)
    def _():
        start(0)
        if n_tiles > 1:
            start(1)

    @pl.when(i + 2 < n_tiles)
    def _():
        wait()
        start(i + 2)
        finish()

    @pl.when(i + 2 >= n_tiles)
    def _():
        wait()
        finish()


def _moe_combine(pos_kt, x, gates, y3, tt):
    t, d = x.shape
    n_tiles = t // tt
    grid_spec = pltpu.PrefetchScalarGridSpec(
        num_scalar_prefetch=1,
        grid=(n_tiles,),
        in_specs=[
            pl.BlockSpec((tt, d), lambda i, p: (i, 0)),
            pl.BlockSpec((tt, LANES), lambda i, p: (i, 0)),
            pl.BlockSpec(memory_space=pl.ANY),
        ],
        out_specs=pl.BlockSpec((tt, d), lambda i, p: (i, 0)),
        scratch_shapes=[pltpu.VMEM((GATHER_DEPTH, MOE_TOPK, tt, PACKED), jnp.uint32),
                        pltpu.SemaphoreType.DMA((GATHER_DEPTH,))],
    )
    return pl.pallas_call(
        functools.partial(_combine_kernel, tt=tt, n_tiles=n_tiles),
        grid_spec=grid_spec,
        out_shape=jax.ShapeDtypeStruct((t, d), F32),
        compiler_params=_params("arbitrary"),
        name="moe_combine",
    )(pos_kt, x, gates, y3)


def _plan_kernel(ids_ref, rank_ref, cnt_ref, carry_ref):
    @pl.when(pl.program_id(0) == 0)
    def _():
        carry_ref[...] = jnp.zeros_like(carry_ref)

    ids = ids_ref[...]
    tt = ids.shape[0]
    lane = lax.broadcasted_iota(jnp.int32, ids.shape, 1)
    oh0 = jnp.where(lane == ids[:, 0:1], 1.0, 0.0)
    oh1 = jnp.where(lane == ids[:, 1:2], 1.0, 0.0)
    earlier = jnp.where(lax.broadcasted_iota(jnp.int32, (tt, tt), 1) < lax.broadcasted_iota(jnp.int32, (tt, tt), 0),
                        1.0, 0.0).astype(BF16)
    before0 = jnp.dot(earlier, oh0.astype(BF16), preferred_element_type=F32)
    before1 = jnp.dot(earlier, oh1.astype(BF16), preferred_element_type=F32)
    tot0 = jnp.sum(oh0, axis=0, keepdims=True)
    carry = carry_ref[...]
    rank0 = jnp.sum(oh0 * (before0 + carry), axis=-1, keepdims=True)
    rank1 = jnp.sum(oh1 * (before1 + tot0 + carry), axis=-1, keepdims=True)
    carry = carry + tot0 + jnp.sum(oh1, axis=0, keepdims=True)
    carry_ref[...] = carry
    cnt_ref[...] = carry
    rank_ref[...] = jnp.where(lane == 0, rank0, jnp.where(lane == 1, rank1, 0.0)).astype(jnp.int32)


def _dispatch_plan(ids, tt):
    t = ids.shape[0]
    n_assign = t * MOE_TOPK
    rank, cnt = pl.pallas_call(
        _plan_kernel,
        grid=(t // tt,),
        in_specs=[pl.BlockSpec((tt, LANES), lambda i: (i, 0))],
        out_specs=[pl.BlockSpec((tt, LANES), lambda i: (i, 0)), pl.BlockSpec((1, LANES), lambda i: (0, 0))],
        out_shape=[jax.ShapeDtypeStruct((t, LANES), jnp.int32), jax.ShapeDtypeStruct((1, LANES), F32)],
        scratch_shapes=[pltpu.VMEM((1, LANES), F32)],
        compiler_params=_params("arbitrary"),
        name="moe_plan",
    )(ids)
    counts = cnt[0, :MOE_EXPERTS].astype(jnp.int32)
    padded = (counts + MOE_BLOCK - 1) // MOE_BLOCK * MOE_BLOCK
    pad_end = jnp.cumsum(padded)
    pad_start = pad_end - padded
    e_ids = ids[:, :MOE_TOPK]
    onehot = e_ids[:, :, None] == jnp.arange(MOE_EXPERTS, dtype=jnp.int32)[None, None, :]
    dest = jnp.sum(jnp.where(onehot, pad_start[None, None, :], 0), axis=-1) + rank[:, :MOE_TOPK]
    n_blocks = -(-n_assign // MOE_BLOCK) + MOE_EXPERTS
    blk_row0 = jnp.arange(n_blocks, dtype=jnp.int32) * MOE_BLOCK
    blk_expert = jnp.minimum(jnp.sum((pad_end[None, :] <= blk_row0[:, None]).astype(jnp.int32), axis=1),
                             MOE_EXPERTS - 1)
    nused = (pad_end[-1:] // MOE_BLOCK).astype(jnp.int32)
    pos_kt = dest.T.reshape(-1).astype(jnp.int32)
    seg = jnp.concatenate([pad_start + counts, pad_end]).astype(jnp.int32)
    return blk_expert, nused, pos_kt, seg


def _pick(n, pref):
    return pref if n % pref == 0 else n


def kernel(x, norm1_g, w_in, qk_norm_g, lambda_qk, attn_head_norm_g, rel_bias, conv_w, conv_b, dt_bias, a_log,
           d_skip, ssm_norm_g, w_attn_out, w_ssm_out, w_out, norm2_g, w_router_group, b_router_group,
           w_router_expert, b_router_expert, w_exp_gate, w_exp_up, w_exp_down):
    b, s, d = x.shape
    t = b * s
    tq = _pick(s, 256)
    bias_win = _bias_windows(rel_bias, s, tq)

    lane = np.arange(DT_COLS)
    grp, rem = lane // DT_LANES_PER_GROUP, lane % DT_LANES_PER_GROUP
    direction, e = rem // SSM_HEADS_PER_GROUP, rem % SSM_HEADS_PER_GROUP
    dt_src = direction * SSM_HEADS + grp * SSM_HEADS_PER_GROUP + e
    pad_lanes = LANES - DT_COLS

    xf = x.reshape(t, d)
    for i in range(DEPTH):
        lam_init = 0.8 - 0.6 * math.exp(-0.3 * i)
        g1 = norm1_g[i][None, :]
        w_tail = lax.slice(w_in, (i, 0, MAIN_COLS), (i + 1, d, w_in.shape[2]))
        w_dt = jnp.pad(w_tail[0, :, :DT_COLS][:, dt_src], ((0, 0), (0, pad_lanes))).astype(BF16)
        dt_b = jnp.pad(dt_bias[i].reshape(-1)[dt_src], (0, pad_lanes))[None, :]
        a_neg = jnp.pad(-jnp.exp(a_log[i].astype(F32)).reshape(-1)[dt_src], (0, pad_lanes))[None, :]

        proj = _norm_matmul(xf, g1, w_in, i, w_tail[0, :, DT_COLS:], BF16, _pick(t, 2048), 1024)
        proj3 = proj.reshape(b, s, PROJ_COLS)
        dt, ac, act = _dt_prep(xf.reshape(b, s, d), g1, w_dt, dt_b, a_neg)

        u = _conv_silu(proj3, conv_w[i], conv_b[i][None, :], 256)
        y_ssd = _ssd_scan(u, dt, ac, act, jnp.repeat(d_skip[i], SSM_HEAD_DIM)[None, :])

        gqk = jnp.tile(qk_norm_g[i], (1, 2))
        attn = _diff_attention(proj3, bias_win, rel_bias.T.astype(F32), gqk, lambda_qk[i],
                               attn_head_norm_g[i][None, :], tq, lam_init)

        w_r = jnp.pad(jnp.concatenate([w_router_group[i], w_router_expert[i]], axis=1),
                      ((0, 0), (0, LANES - MOE_GROUPS - MOE_EXPERTS)))
        b_r = jnp.pad(jnp.concatenate([b_router_group[i], b_router_expert[i]]),
                      (0, LANES - MOE_GROUPS - MOE_EXPERTS))[None, :]
        x_mid, hn, ids, gates = _post_mixer(
            attn.reshape(t, ATTN_WIDTH), y_ssd.reshape(t, SSM_INNER), proj, xf, ssm_norm_g[i][None, :],
            w_attn_out[i].astype(BF16), w_ssm_out[i].astype(BF16), w_out[i].astype(BF16), norm2_g[i][None, :],
            w_r, b_r, _pick(t, 256))

        blk_expert, nused, pos_kt, seg = _dispatch_plan(ids, _pick(t, 256))
        y_pad = _moe_experts(blk_expert, pos_kt, nused, seg, hn, w_exp_gate, w_exp_up, w_exp_down, i)
        xf = _moe_combine(pos_kt, x_mid, gates, y_pad, _pick(t, 128))
    return xf.reshape(b, s, d)
```

```python
import functools
import math

import jax
import jax.numpy as jnp
import numpy as np
from jax import lax
from jax.experimental import pallas as pl
from jax.experimental.pallas import tpu as pltpu

D_MODEL = 1024
DEPTH = 2
ATTN_HEADS = 8
ATTN_HEAD_DIM = 64
ATTN_V_DIM = 2 * ATTN_HEAD_DIM
ATTN_WIDTH = ATTN_HEADS * ATTN_V_DIM
NUM_BUCKETS = 32
MAX_DISTANCE = 128
SSM_INNER = 2 * D_MODEL
SSM_HEAD_DIM = 64
SSM_HEADS = SSM_INNER // SSM_HEAD_DIM
SSM_GROUPS = 4
SSM_HEADS_PER_GROUP = SSM_HEADS // SSM_GROUPS
SSM_STATE = 128
SSM_CONV = 5
SSM_CHUNK = 128
SSM_CONV_CH = SSM_INNER + 2 * SSM_GROUPS * SSM_STATE
MOE_GROUPS = 4
MOE_EXPERTS_PER_GROUP = 8
MOE_EXPERTS = MOE_GROUPS * MOE_EXPERTS_PER_GROUP
MOE_TOPK = 2
MOE_FF = 512
MOE_BLOCK = 128
RMS_EPS = 1e-6

LANES = 128
COL_Q = 0
COL_K = COL_Q + ATTN_WIDTH
COL_V = COL_K + ATTN_WIDTH
COL_Z = COL_V + ATTN_WIDTH
COL_XBC = COL_Z + SSM_INNER
MAIN_COLS = COL_XBC + SSM_CONV_CH
DT_COLS = 2 * SSM_HEADS
COL_GATES = MAIN_COLS + DT_COLS
COL_GA = MAIN_COLS
COL_GM = COL_GA + D_MODEL
PROJ_COLS = COL_GM + D_MODEL
DT_LANES_PER_GROUP = 2 * SSM_HEADS_PER_GROUP
PACKED = D_MODEL // 2

VMEM_LIMIT_BYTES = 56 * 1024 * 1024
LOG2E = 1.4426950408889634

F32 = jnp.float32
BF16 = jnp.bfloat16


def _params(*semantics):
    return pltpu.CompilerParams(dimension_semantics=semantics, vmem_limit_bytes=VMEM_LIMIT_BYTES)


def _rms(x, g):
    ms = jnp.mean(x * x, axis=-1, keepdims=True)
    return x * lax.rsqrt(ms + RMS_EPS) * g


def _norm_matmul_kernel(x_ref, g_ref, w_ref, wg_ref, o_ref, hn_ref, *, n_main):
    j = pl.program_id(1)

    @pl.when(j == 0)
    def _():
        hn_ref[...] = _rms(x_ref[...], g_ref[...]).astype(hn_ref.dtype)

    @pl.when(j < n_main)
    def _():
        o_ref[...] = jnp.dot(hn_ref[...], w_ref[...].astype(BF16), preferred_element_type=F32).astype(o_ref.dtype)

    @pl.when(j >= n_main)
    def _():
        o_ref[...] = jnp.dot(hn_ref[...], wg_ref[...].astype(BF16), preferred_element_type=F32).astype(o_ref.dtype)


def _norm_matmul(x, g, w, layer, w_gates, out_dtype, tm, tn):
    t, d = x.shape
    n_main = MAIN_COLS // tn
    n_gate = w_gates.shape[1] // tn
    return pl.pallas_call(
        functools.partial(_norm_matmul_kernel, n_main=n_main),
        grid=(t // tm, n_main + n_gate),
        in_specs=[
            pl.BlockSpec((tm, d), lambda i, j: (i, 0)),
            pl.BlockSpec((1, d), lambda i, j: (0, 0)),
            pl.BlockSpec((None, d, tn), lambda i, j: (layer, 0, jnp.minimum(j, n_main - 1))),
            pl.BlockSpec((d, tn), lambda i, j: (0, jnp.maximum(j - n_main, 0))),
        ],
        out_specs=pl.BlockSpec((tm, tn), lambda i, j: (i, j)),
        out_shape=jax.ShapeDtypeStruct((t, (n_main + n_gate) * tn), out_dtype),
        scratch_shapes=[pltpu.VMEM((tm, d), BF16)],
        compiler_params=_params("parallel", "arbitrary"),
        name="in_proj",
    )(x, g, w, w_gates)


def _dt_prep_kernel(x_ref, g_ref, w_ref, bias_ref, a_ref, ac_ref, act_ref, dtt_ref):
    hn = _rms(x_ref[...], g_ref[...]).astype(BF16)
    raw = jnp.dot(hn, w_ref[...], preferred_element_type=F32) + bias_ref[...]
    dt = jnp.maximum(raw, 0.0) + jnp.log1p(jnp.exp(-jnp.abs(raw)))
    a = dt * (a_ref[...] * LOG2E)
    s = a.shape[0]
    pos = lax.broadcasted_iota(jnp.int32, a.shape, 0) & (SSM_CHUNK - 1)
    pre = a
    suf = a
    k = 1
    while k < SSM_CHUNK:
        pre = pre + jnp.where(pos >= k, pltpu.roll(pre, k, 0), 0.0)
        suf = suf + jnp.where(pos < SSM_CHUNK - k, pltpu.roll(suf, s - k, 0), 0.0)
        k *= 2
    lane = lax.broadcasted_iota(jnp.int32, a.shape, 1)
    is_fwd = (lane & (DT_LANES_PER_GROUP - 1)) < SSM_HEADS_PER_GROUP
    ac = jnp.where(is_fwd, pre, suf)
    act_ref[...] = ac.T
    dtt_ref[...] = dt.T
    for grp in range(SSM_GROUPS):
        shift = (LANES - grp * DT_LANES_PER_GROUP) % LANES
        ac_ref[:, grp * LANES:(grp + 1) * LANES] = ac if shift == 0 else pltpu.roll(ac, shift, 1)


def _dt_prep(x3, g, w_dt, bias, a_neg):
    b, s, d = x3.shape
    out = jax.ShapeDtypeStruct((b, s, SSM_GROUPS * LANES), F32)
    out_t = jax.ShapeDtypeStruct((b, LANES, s), F32)
    vec = pl.BlockSpec((1, LANES), lambda i: (0, 0))
    return pl.pallas_call(
        _dt_prep_kernel,
        grid=(b,),
        in_specs=[
            pl.BlockSpec((None, s, d), lambda i: (i, 0, 0)),
            pl.BlockSpec((1, d), lambda i: (0, 0)),
            pl.BlockSpec((d, LANES), lambda i: (0, 0)),
            vec,
            vec,
        ],
        out_specs=[
            pl.BlockSpec((None, s, SSM_GROUPS * LANES), lambda i: (i, 0, 0)),
            pl.BlockSpec((None, LANES, s), lambda i: (i, 0, 0)),
            pl.BlockSpec((None, LANES, s), lambda i: (i, 0, 0)),
        ],
        out_shape=[out, out_t, out_t],
        compiler_params=_params("parallel"),
        name="dt_prep",
    )(x3, g, w_dt, bias, a_neg)


def _conv_kernel(x_ref, w_ref, b_ref, o_ref):
    xf = x_ref[...].astype(F32)
    s = xf.shape[0]
    row = lax.broadcasted_iota(jnp.int32, xf.shape, 0)
    pad = SSM_CONV // 2
    acc = xf * w_ref[pad:pad + 1, :] + b_ref[...]
    for j in range(SSM_CONV):
        shift = pad - j
        if shift == 0:
            continue
        rolled = pltpu.roll(xf, shift % s, 0)
        valid = (row >= shift) & (row < s + shift)
        acc = acc + jnp.where(valid, rolled, 0.0) * w_ref[j:j + 1, :]
    o_ref[...] = (acc * jax.nn.sigmoid(acc)).astype(o_ref.dtype)


def _conv_silu(proj3, conv_w, conv_b, tc):
    b, s, _ = proj3.shape
    col0 = COL_XBC // tc
    return pl.pallas_call(
        _conv_kernel,
        grid=(b, SSM_CONV_CH // tc),
        in_specs=[
            pl.BlockSpec((None, s, tc), lambda i, c: (i, 0, col0 + c)),
            pl.BlockSpec((SSM_CONV, tc), lambda i, c: (0, c)),
            pl.BlockSpec((1, tc), lambda i, c: (0, c)),
        ],
        out_specs=pl.BlockSpec((None, s, tc), lambda i, c: (i, 0, c)),
        out_shape=jax.ShapeDtypeStruct((b, s, SSM_CONV_CH), BF16),
        compiler_params=_params("parallel", "parallel"),
        name="conv_silu",
    )(proj3, conv_w, conv_b)


def _rel_bucket(rel):
    half = NUM_BUCKETS // 2
    max_exact = half // 2
    ret = jnp.where(rel > 0, half, 0)
    n = jnp.abs(rel)
    n_safe = jnp.maximum(n, 1).astype(F32)
    large = max_exact + (jnp.log(n_safe / max_exact) / math.log(MAX_DISTANCE / max_exact)
                         * (half - max_exact)).astype(jnp.int32)
    large = jnp.minimum(large, half - 1)
    return ret + jnp.where(n < max_exact, n, large)


def _bias_kernel(relb_ref, bid_ref, o_ref, *, tq):
    h = pl.program_id(0)
    bid = bid_ref[...]
    fv = jnp.zeros(bid.shape, F32)
    for bkt in range(NUM_BUCKETS):
        fv = jnp.where(bid == bkt, relb_ref[h, bkt], fv)
    g = jnp.broadcast_to(fv, o_ref.shape)
    row = lax.broadcasted_iota(jnp.int32, o_ref.shape, 0)
    k = 1
    while k < tq:
        g = jnp.where((row & k) != 0, pltpu.roll(g, k, 1), g)
        k *= 2
    o_ref[...] = g


def _bias_windows(rel_bias, s, tq):
    width = 2 * s
    m = jnp.arange(width, dtype=jnp.int32)
    bid = _rel_bucket(((m + tq) % width) - s).astype(jnp.int32)[None, :]
    return pl.pallas_call(
        functools.partial(_bias_kernel, tq=tq),
        grid=(ATTN_HEADS,),
        in_specs=[
            pl.BlockSpec(memory_space=pltpu.SMEM),
            pl.BlockSpec((1, width), lambda h: (0, 0)),
        ],
        out_specs=pl.BlockSpec((None, tq, width), lambda h: (h, 0, 0)),
        out_shape=jax.ShapeDtypeStruct((ATTN_HEADS, tq, width), F32),
        compiler_params=_params("parallel"),
        name="rel_bias_windows",
    )(rel_bias.T.astype(F32), bid)


SOFTMAX_SHIFT_SLACK = 40.0
BF16_ROUNDING_SLACK = 1.01


def _attn_kernel(relb_ref, q_ref, k_ref, v_ref, bias_ref, gqk_ref, lamqk_ref, gh_ref, o_ref, q0_ref, q1_ref, kn_ref,
                 vx_ref, scal_ref, *, tq, lam_init):
    h = pl.program_id(0)
    qi = pl.program_id(2)
    s = k_ref.shape[0]
    lo = lax.broadcasted_iota(jnp.int32, (1, LANES), 1) < ATTN_HEAD_DIM

    def half_sums(x2):
        s_all = jnp.sum(x2, axis=-1, keepdims=True)
        s_lo = jnp.sum(jnp.where(lo, x2, 0.0), axis=-1, keepdims=True)
        return s_lo, jnp.maximum(s_all - s_lo, 0.0)

    def half_norm(x, g):
        s_lo, s_hi = half_sums(x * x)
        ms = jnp.where(lo, s_lo, s_hi) * (1.0 / ATTN_HEAD_DIM)
        return x * lax.rsqrt(ms + RMS_EPS) * g

    @pl.when(qi == 0)
    def _():
        kn_ref[...] = half_norm(k_ref[...].astype(F32), gqk_ref[1:2, :]).astype(BF16)
        vx_ref[:, :LANES] = v_ref[...]
        vx_ref[:, LANES:] = jnp.ones((s, LANES), BF16)
        bmax = relb_ref[h, 0]
        bmin = relb_ref[h, 0]
        for bkt in range(1, NUM_BUCKETS):
            bmax = jnp.maximum(bmax, relb_ref[h, bkt])
            bmin = jnp.minimum(bmin, relb_ref[h, bkt])
        qn = (half_norm(q_ref[...].astype(F32), gqk_ref[0:1, :]) * (ATTN_HEAD_DIM ** -0.5)).astype(BF16)
        q0_ref[...] = jnp.where(lo, qn, jnp.zeros_like(qn))
        q1_ref[...] = jnp.where(lo, jnp.zeros_like(qn), qn)
        gains = jnp.abs(gqk_ref[...])
        reach = (BF16_ROUNDING_SLACK * ATTN_HEAD_DIM ** 0.5) * jnp.max(gains[0:1, :]) * jnp.max(gains[1:2, :])
        scal_ref[0] = reach + bmax
        scal_ref[1] = 2.0 * reach + (bmax - bmin)

    rows = pl.ds(pl.multiple_of(qi * tq, tq), tq)
    q0 = q0_ref[rows, :]
    q1 = q1_ref[rows, :]
    shift = scal_ref[0]
    slack = scal_ref[1]
    c0 = pl.multiple_of(s - (qi + 1) * tq, tq)
    lq = lamqk_ref[...]
    lam = (jnp.exp(jnp.sum(lq[0:1] * lq[1:2], axis=-1, keepdims=True))
           - jnp.exp(jnp.sum(lq[2:3] * lq[3:4], axis=-1, keepdims=True)) + lam_init)

    def logits_of(qm):
        return lax.dot_general(qm, kn_ref[...], (((1,), (1,)), ((), ())),
                               preferred_element_type=F32) + bias_ref[:, pl.ds(c0, s)]

    def finish(out):
        o_ref[...] = (_rms(out, gh_ref[...]) * (1.0 - lam_init)).astype(o_ref.dtype)

    @pl.when(slack <= SOFTMAX_SHIFT_SLACK)
    def _():
        def weighted_v(qm):
            p = jnp.exp(logits_of(qm) - shift).astype(BF16)
            return jnp.dot(p, vx_ref[...], preferred_element_type=F32)

        o0 = weighted_v(q0)
        o1 = weighted_v(q1)
        finish(o0[:, :LANES] / o0[:, LANES:LANES + 1] - o1[:, :LANES] * (lam / o1[:, LANES:LANES + 1]))

    @pl.when(slack > SOFTMAX_SHIFT_SLACK)
    def _():
        def softmax_map(qm):
            logits = logits_of(qm)
            p = jnp.exp(logits - jnp.max(logits, axis=-1, keepdims=True))
            return p, jnp.sum(p, axis=-1, keepdims=True)

        p0, l0 = softmax_map(q0)
        p1, l1 = softmax_map(q1)
        attn = p0 * (1.0 / l0) - p1 * (lam / l1)
        finish(jnp.dot(attn.astype(BF16), v_ref[...], preferred_element_type=F32))


def _diff_attention(proj3, bias_win, relb_t, gqk, lamqk, gh, tq, lam_init):
    b, s, _ = proj3.shape
    width = bias_win.shape[-1]
    return pl.pallas_call(
        functools.partial(_attn_kernel, tq=tq, lam_init=lam_init),
        grid=(ATTN_HEADS, b, s // tq),
        in_specs=[
            pl.BlockSpec(memory_space=pltpu.SMEM),
            pl.BlockSpec((None, s, LANES), lambda h, i, q: (i, 0, COL_Q // LANES + h)),
            pl.BlockSpec((None, s, LANES), lambda h, i, q: (i, 0, COL_K // LANES + h)),
            pl.BlockSpec((None, s, LANES), lambda h, i, q: (i, 0, COL_V // LANES + h)),
            pl.BlockSpec((None, tq, width), lambda h, i, q: (h, 0, 0)),
            pl.BlockSpec((2, LANES), lambda h, i, q: (0, 0)),
            pl.BlockSpec((4, ATTN_HEAD_DIM), lambda h, i, q: (0, 0)),
            pl.BlockSpec((1, LANES), lambda h, i, q: (0, 0)),
        ],
        out_specs=pl.BlockSpec((None, tq, LANES), lambda h, i, q: (i, q, h)),
        out_shape=jax.ShapeDtypeStruct((b, s, ATTN_WIDTH), BF16),
        scratch_shapes=[pltpu.VMEM((s, LANES), BF16), pltpu.VMEM((s, LANES), BF16), pltpu.VMEM((s, LANES), BF16),
                        pltpu.VMEM((s, 2 * LANES), BF16), pltpu.SMEM((2,), F32)],
        compiler_params=_params("parallel", "parallel", "arbitrary"),
        name="diff_attention",
    )(relb_t, proj3, proj3, proj3, bias_win, gqk, lamqk, gh)


def _ssd_kernel(x_ref, b_ref, c_ref, ac_ref, act_ref, dtt_ref, dsk_ref, y_ref, st_ref, acc_ref, bt_ref):
    L = SSM_CHUNK
    s = x_ref.shape[0]
    n_chunks = s // L
    row = lax.broadcasted_iota(jnp.int32, (L, L), 0)
    col = lax.broadcasted_iota(jnp.int32, (L, L), 1)
    lo = lax.broadcasted_iota(jnp.int32, (L, LANES), 1) < SSM_HEAD_DIM
    lo1 = lo[0:1, :]
    n_pairs = SSM_HEADS_PER_GROUP // 2

    bt_ref[...] = b_ref[...].astype(F32).T.astype(BF16)

    def one_chunk(c, backward):
        d = 1 if backward else 0
        rows = pl.ds(pl.multiple_of(c * L, L), L)
        off = SSM_HEADS_PER_GROUP * d
        tot_row = 0 if backward else L - 1
        mask = (row <= col) if backward else (row >= col)
        xb = x_ref[rows, :]
        cc = c_ref[rows, :]
        acc = ac_ref[rows, :]
        act = act_ref[:, rows]
        dtt = dtt_ref[:, rows]
        bt = bt_ref[:, rows].astype(F32)
        cb = lax.dot_general(cc, b_ref[rows, :], (((1,), (1,)), ((), ())), preferred_element_type=F32)

        ys = []
        for j in range(n_pairs):
            lanes = slice(j * LANES, (j + 1) * LANES)
            x_pair = xb[:, lanes]
            y_off = jnp.dot(cc, st_ref[d, :, lanes].astype(BF16), preferred_element_type=F32)
            diag, stn, acol, tot = [], [], [], []
            for e in (2 * j, 2 * j + 1):
                a_col = jnp.broadcast_to(acc[:, off + e:off + e + 1], (L, LANES))
                a_row = act[off + e:off + e + 1, :]
                dt_row = dtt[off + e:off + e + 1, :]
                tot_e = acc[tot_row:tot_row + 1, off + e:off + e + 1]
                decay = jnp.where(mask, jnp.exp2(a_col - a_row), 0.0)
                diag.append(jnp.dot((cb * decay * dt_row).astype(BF16), x_pair, preferred_element_type=F32))
                w_row = dt_row * jnp.exp2(tot_e - a_row)
                stn.append(jnp.dot((bt * w_row).astype(BF16), x_pair, preferred_element_type=F32))
                acol.append(a_col)
                tot.append(tot_e)
            ys.append(jnp.where(lo, diag[0], diag[1]) + y_off * jnp.exp2(jnp.where(lo, acol[0], acol[1])))
            st_ref[d, :, lanes] = (st_ref[d, :, lanes] * jnp.exp2(jnp.where(lo1, tot[0], tot[1]))
                                   + jnp.where(lo, stn[0], stn[1]))
        y = jnp.concatenate(ys, axis=1)
        if not backward:
            y = y + xb.astype(F32) * dsk_ref[...]
        acc_ref[d, rows, :] = y

    st_ref[...] = jnp.zeros_like(st_ref)

    def body(c, carry):
        one_chunk(c, False)
        one_chunk(n_chunks - 1 - c, True)
        return carry

    lax.fori_loop(0, n_chunks, body, 0)
    y_ref[...] = (acc_ref[0] + acc_ref[1]).astype(y_ref.dtype)


def _ssd_scan(u, ac_g, act_g, dtt_g, dskip):
    b, s, _ = u.shape
    gw = SSM_HEADS_PER_GROUP * SSM_HEAD_DIM
    xcols = SSM_INNER // SSM_STATE
    nl = DT_LANES_PER_GROUP
    return pl.pallas_call(
        _ssd_kernel,
        grid=(b, SSM_GROUPS),
        in_specs=[
            pl.BlockSpec((None, s, gw), lambda i, g: (i, 0, g)),
            pl.BlockSpec((None, s, SSM_STATE), lambda i, g: (i, 0, xcols + g)),
            pl.BlockSpec((None, s, SSM_STATE), lambda i, g: (i, 0, xcols + SSM_GROUPS + g)),
            pl.BlockSpec((None, s, LANES), lambda i, g: (i, 0, g)),
            pl.BlockSpec((None, nl, s), lambda i, g: (i, g, 0)),
            pl.BlockSpec((None, nl, s), lambda i, g: (i, g, 0)),
            pl.BlockSpec((1, gw), lambda i, g: (0, g)),
        ],
        out_specs=pl.BlockSpec((None, s, gw), lambda i, g: (i, 0, g)),
        out_shape=jax.ShapeDtypeStruct((b, s, SSM_INNER), BF16),
        scratch_shapes=[pltpu.VMEM((2, SSM_STATE, gw), F32), pltpu.VMEM((2, s, gw), F32),
                        pltpu.VMEM((SSM_STATE, s), BF16)],
        compiler_params=_params("parallel", "parallel"),
        name="ssd_scan",
    )(u, u, u, ac_g, act_g, dtt_g, dskip)


def _route(logits):
    lane = lax.broadcasted_iota(jnp.int32, logits.shape, 1)
    lane_f = lane.astype(F32)
    neg = -jnp.inf
    big = float(LANES)
    is_g = lane < MOE_GROUPS
    gl = jnp.where(is_g, logits, neg)
    gmax = jnp.max(gl, axis=-1, keepdims=True)
    g_idx = jnp.min(jnp.where(gl == gmax, lane_f, big), axis=-1, keepdims=True)
    g_gate = 1.0 / jnp.sum(jnp.where(is_g, jnp.exp(logits - gmax), 0.0), axis=-1, keepdims=True)
    first = MOE_GROUPS + MOE_EXPERTS_PER_GROUP * g_idx
    in_grp = (lane_f >= first) & (lane_f < first + MOE_EXPERTS_PER_GROUP)
    el = jnp.where(in_grp, logits, neg)
    v1 = jnp.max(el, axis=-1, keepdims=True)
    i1 = jnp.min(jnp.where(el == v1, lane_f, big), axis=-1, keepdims=True)
    el2 = jnp.where(lane_f == i1, neg, el)
    v2 = jnp.max(el2, axis=-1, keepdims=True)
    i2 = jnp.min(jnp.where(el2 == v2, lane_f, big), axis=-1, keepdims=True)
    r = jnp.exp(v2 - v1)
    p1 = 1.0 / (1.0 + r)
    p2 = r * p1
    ids = jnp.where(lane == 0, i1 - MOE_GROUPS, jnp.where(lane == 1, i2 - MOE_GROUPS, 0.0)).astype(jnp.int32)
    gates = jnp.where(lane == 0, p1 * g_gate, jnp.where(lane == 1, p2 * g_gate, 0.0))
    return ids, gates


_HI_HALF = 0xFFFF0000


def _pack_bf16_pairs(x):
    n = x.shape[1] // 2
    bits = lax.bitcast_convert_type(x.astype(BF16).astype(F32), jnp.uint32)
    return (bits[:, n:] & jnp.uint32(_HI_HALF)) | (bits[:, :n] >> 16)


def _unpack_bf16_pairs(w):
    lo = lax.bitcast_convert_type(w << 16, F32)
    hi = lax.bitcast_convert_type(w & jnp.uint32(_HI_HALF), F32)
    return jnp.concatenate([lo, hi], axis=1)


def _post_kernel(attn_ref, y_ref, z0_ref, z1_ref, ga_ref, gm_ref, x_ref, gs_ref, wa_ref, ws_ref, wo_ref, g2_ref,
                 wr_ref, br_ref, xo_ref, hn_ref, id_ref, gate_ref):
    ya = jnp.dot(attn_ref[...], wa_ref[...], preferred_element_type=F32)
    z = jnp.concatenate([z0_ref[...], z1_ref[...]], axis=1).astype(F32)
    gated = y_ref[...].astype(F32) * (z * jax.nn.sigmoid(z))
    ym = jnp.dot(_rms(gated, gs_ref[...]).astype(BF16), ws_ref[...], preferred_element_type=F32)
    merged = (jax.nn.sigmoid(ga_ref[...].astype(F32)) * ya + jax.nn.sigmoid(gm_ref[...].astype(F32)) * ym)
    x_new = x_ref[...] + jnp.dot(merged.astype(BF16), wo_ref[...], preferred_element_type=F32)
    xo_ref[...] = x_new
    hn = _rms(x_new, g2_ref[...])
    hn_ref[:, 0, :] = _pack_bf16_pairs(hn)
    hn_hi = hn.astype(BF16)
    hn_lo = (hn - hn_hi.astype(F32)).astype(BF16)
    wr = wr_ref[...]
    wr_hi = wr.astype(BF16)
    wr_lo = (wr - wr_hi.astype(F32)).astype(BF16)
    logits = (jnp.dot(hn_hi, wr_hi, preferred_element_type=F32) + jnp.dot(hn_lo, wr_hi, preferred_element_type=F32)
              + jnp.dot(hn_hi, wr_lo, preferred_element_type=F32) + br_ref[...])
    ids, gates = _route(logits)
    id_ref[...] = ids
    gate_ref[...] = gates


def _post_mixer(attn, y, proj, x, gs, wa, ws, wo, g2, wr, br, tm):
    t, d = x.shape
    row = lambda w, c: pl.BlockSpec((tm, w), lambda i: (i, c))
    full = lambda a: pl.BlockSpec(a.shape, lambda i: (0,) * a.ndim)
    return pl.pallas_call(
        _post_kernel,
        grid=(t // tm,),
        in_specs=[
            row(ATTN_WIDTH, 0),
            row(SSM_INNER, 0),
            row(D_MODEL, COL_Z // D_MODEL),
            row(D_MODEL, COL_Z // D_MODEL + 1),
            row(D_MODEL, COL_GA // D_MODEL),
            row(D_MODEL, COL_GM // D_MODEL),
            row(D_MODEL, 0),
            full(gs), full(wa), full(ws), full(wo), full(g2), full(wr), full(br),
        ],
        out_specs=[row(D_MODEL, 0), pl.BlockSpec((tm, 1, PACKED), lambda i: (i, 0, 0)), row(LANES, 0), row(LANES, 0)],
        out_shape=[
            jax.ShapeDtypeStruct((t, d), F32),
            jax.ShapeDtypeStruct((t, 1, PACKED), jnp.uint32),
            jax.ShapeDtypeStruct((t, LANES), jnp.int32),
            jax.ShapeDtypeStruct((t, LANES), F32),
        ],
        compiler_params=_params("parallel"),
        name="post_mixer",
    )(attn, y, proj, proj, proj, proj, x, gs, wa, ws, wo, g2, wr, br)


GATHER_DEPTH = 3


def _start_rows(idx_ref, base, src_hbm, dst_ref, sem, rows=None):
    for r in (range(dst_ref.shape[0]) if rows is None else rows):
        pltpu.make_async_copy(src_hbm.at[idx_ref[base + r]], dst_ref.at[pl.ds(r, 1), :], sem).start(priority=r % 2)


def _wait_buffer(dst_ref, sem):
    pltpu.make_async_copy(dst_ref, dst_ref, sem).wait()


def _moe_kernel(blk_e_ref, pos_ref, nused_ref, seg_ref, hn_hbm, wg_ref, wu_ref, wd_ref, y_ref, src_ref, xg_ref,
                wgb_ref, wub_ref, wdb_ref, sem, *, n_tokens):
    i = pl.program_id(0)
    nused = nused_ref[0]
    quarter = MOE_BLOCK // 4

    def start(blk, part=None):
        slot = blk % GATHER_DEPTH
        rows = range(MOE_BLOCK) if part is None else range(part * quarter, (part + 1) * quarter)
        _start_rows(src_ref, blk * MOE_BLOCK, hn_hbm, xg_ref.at[slot], sem.at[slot], rows)

    @pl.when(i == 0)
    def _():
        def clear(j, carry):
            src_ref[j] = 0
            return carry

        for e in range(MOE_EXPERTS):
            lax.fori_loop(seg_ref[e], seg_ref[MOE_EXPERTS + e], clear, 0)
        for k in range(MOE_TOPK):
            def fill(tok, carry):
                src_ref[pos_ref[k * n_tokens + tok]] = tok
                return carry

            lax.fori_loop(0, n_tokens, fill, 0, unroll=8)
        start(0)

    @pl.when((i == 0) & (nused > 1))
    def _():
        start(1)

    new_expert = (i == 0) | (blk_e_ref[i] != blk_e_ref[jnp.maximum(i - 1, 0)])

    @pl.when((i < nused) & new_expert)
    def _():
        wgb_ref[...] = wg_ref[...].astype(BF16)
        wub_ref[...] = wu_ref[...].astype(BF16)
        wdb_ref[...] = wd_ref[...].astype(BF16)

    def compute(prefetch):
        xb = _unpack_bf16_pairs(xg_ref[i % GATHER_DEPTH]).astype(BF16)
        prefetch(0)
        hg = jnp.dot(xb, wgb_ref[...], preferred_element_type=F32)
        prefetch(1)
        hu = jnp.dot(xb, wub_ref[...], preferred_element_type=F32)
        prefetch(2)
        hid = (hg * jax.nn.sigmoid(hg) * hu).astype(BF16)
        y = jnp.dot(hid, wdb_ref[...], preferred_element_type=F32)
        prefetch(3)
        y_ref[:, 0, :] = _pack_bf16_pairs(y)

    def wait():
        slot = i % GATHER_DEPTH
        _wait_buffer(xg_ref.at[slot], sem.at[slot])

    @pl.when(i + 2 < nused)
    def _():
        wait()
        compute(lambda part: start(i + 2, part))

    @pl.when((i < nused) & (i + 2 >= nused))
    def _():
        wait()
        compute(lambda part: None)

    @pl.when(i >= nused)
    def _():
        y_ref[...] = jnp.zeros_like(y_ref)


def _moe_experts(blk_expert, pos_kt, nused, seg, hn3, wg, wu, wd, layer):
    n_blocks = blk_expert.shape[0]
    d = wg.shape[2]
    n_tokens = hn3.shape[0]
    grid_spec = pltpu.PrefetchScalarGridSpec(
        num_scalar_prefetch=4,
        grid=(n_blocks,),
        in_specs=[
            pl.BlockSpec(memory_space=pl.ANY),
            pl.BlockSpec((None, None, d, MOE_FF), lambda i, be, st, nu, sg: (layer, be[i], 0, 0)),
            pl.BlockSpec((None, None, d, MOE_FF), lambda i, be, st, nu, sg: (layer, be[i], 0, 0)),
            pl.BlockSpec((None, None, MOE_FF, d), lambda i, be, st, nu, sg: (layer, be[i], 0, 0)),
        ],
        out_specs=pl.BlockSpec((MOE_BLOCK, 1, PACKED), lambda i, be, st, nu, sg: (i, 0, 0)),
        scratch_shapes=[
            pltpu.SMEM((n_blocks * MOE_BLOCK,), jnp.int32),
            pltpu.VMEM((GATHER_DEPTH, MOE_BLOCK, PACKED), jnp.uint32),
            pltpu.VMEM((d, MOE_FF), BF16),
            pltpu.VMEM((d, MOE_FF), BF16),
            pltpu.VMEM((MOE_FF, d), BF16),
            pltpu.SemaphoreType.DMA((GATHER_DEPTH,)),
        ],
    )
    return pl.pallas_call(
        functools.partial(_moe_kernel, n_tokens=n_tokens),
        grid_spec=grid_spec,
        out_shape=jax.ShapeDtypeStruct((n_blocks * MOE_BLOCK, 1, PACKED), jnp.uint32),
        compiler_params=_params("arbitrary"),
        name="moe_experts",
    )(blk_expert, pos_kt, nused, seg, hn3, wg, wu, wd)


def _combine_kernel(pos_ref, x_ref, gate_ref, y_hbm, o_ref, yg_ref, sem, *, tt, n_tiles):
    i = pl.program_id(0)

    def start(tile):
        slot = tile % GATHER_DEPTH
        for k in range(MOE_TOPK):
            _start_rows(pos_ref, (k * n_tiles + tile) * tt, y_hbm, yg_ref.at[slot, k], sem.at[slot])

    def finish():
        gates = gate_ref[...]
        slot = i % GATHER_DEPTH
        o_ref[...] = (x_ref[...] + gates[:, 0:1] * _unpack_bf16_pairs(yg_ref[slot, 0])
                      + gates[:, 1:2] * _unpack_bf16_pairs(yg_ref[slot, 1]))

    def wait():
        slot = i % GATHER_DEPTH
        _wait_buffer(yg_ref.at[slot], sem.at[slot])

    @pl.when(i == 0)
    def _():
        start(0)
        if n_tiles > 1:
            start(1)

    @pl.when(i + 2 < n_tiles)
    def _():
        wait()
        start(i + 2)
        finish()

    @pl.when(i + 2 >= n_tiles)
    def _():
        wait()
        finish()


def _moe_combine(pos_kt, x, gates, y3, tt):
    t, d = x.shape
    n_tiles = t // tt
    grid_spec = pltpu.PrefetchScalarGridSpec(
        num_scalar_prefetch=1,
        grid=(n_tiles,),
        in_specs=[
            pl.BlockSpec((tt, d), lambda i, p: (i, 0)),
            pl.BlockSpec((tt, LANES), lambda i, p: (i, 0)),
            pl.BlockSpec(memory_space=pl.ANY),
        ],
        out_specs=pl.BlockSpec((tt, d), lambda i, p: (i, 0)),
        scratch_shapes=[pltpu.VMEM((GATHER_DEPTH, MOE_TOPK, tt, PACKED), jnp.uint32),
                        pltpu.SemaphoreType.DMA((GATHER_DEPTH,))],
    )
    return pl.pallas_call(
        functools.partial(_combine_kernel, tt=tt, n_tiles=n_tiles),
        grid_spec=grid_spec,
        out_shape=jax.ShapeDtypeStruct((t, d), F32),
        compiler_params=_params("arbitrary"),
        name="moe_combine",
    )(pos_kt, x, gates, y3)


def _plan_kernel(ids_ref, rank_ref, cnt_ref, carry_ref):
    @pl.when(pl.program_id(0) == 0)
    def _():
        carry_ref[...] = jnp.zeros_like(carry_ref)

    ids = ids_ref[...]
    tt = ids.shape[0]
    lane = lax.broadcasted_iota(jnp.int32, ids.shape, 1)
    oh0 = jnp.where(lane == ids[:, 0:1], 1.0, 0.0)
    oh1 = jnp.where(lane == ids[:, 1:2], 1.0, 0.0)
    earlier = jnp.where(lax.broadcasted_iota(jnp.int32, (tt, tt), 1) < lax.broadcasted_iota(jnp.int32, (tt, tt), 0),
                        1.0, 0.0).astype(BF16)
    before0 = jnp.dot(earlier, oh0.astype(BF16), preferred_element_type=F32)
    before1 = jnp.dot(earlier, oh1.astype(BF16), preferred_element_type=F32)
    tot0 = jnp.sum(oh0, axis=0, keepdims=True)
    carry = carry_ref[...]
    rank0 = jnp.sum(oh0 * (before0 + carry), axis=-1, keepdims=True)
    rank1 = jnp.sum(oh1 * (before1 + tot0 + carry), axis=-1, keepdims=True)
    carry = carry + tot0 + jnp.sum(oh1, axis=0, keepdims=True)
    carry_ref[...] = carry
    cnt_ref[...] = carry
    rank_ref[...] = jnp.where(lane == 0, rank0, jnp.where(lane == 1, rank1, 0.0)).astype(jnp.int32)


def _dispatch_plan(ids, tt):
    t = ids.shape[0]
    n_assign = t * MOE_TOPK
    rank, cnt = pl.pallas_call(
        _plan_kernel,
        grid=(t // tt,),
        in_specs=[pl.BlockSpec((tt, LANES), lambda i: (i, 0))],
        out_specs=[pl.BlockSpec((tt, LANES), lambda i: (i, 0)), pl.BlockSpec((1, LANES), lambda i: (0, 0))],
        out_shape=[jax.ShapeDtypeStruct((t, LANES), jnp.int32), jax.ShapeDtypeStruct((1, LANES), F32)],
        scratch_shapes=[pltpu.VMEM((1, LANES), F32)],
        compiler_params=_params("arbitrary"),
        name="moe_plan",
    )(ids)
    counts = cnt[0, :MOE_EXPERTS].astype(jnp.int32)
    padded = (counts + MOE_BLOCK - 1) // MOE_BLOCK * MOE_BLOCK
    pad_end = jnp.cumsum(padded)
    pad_start = pad_end - padded
    e_ids = ids[:, :MOE_TOPK]
    onehot = e_ids[:, :, None] == jnp.arange(MOE_EXPERTS, dtype=jnp.int32)[None, None, :]
    dest = jnp.sum(jnp.where(onehot, pad_start[None, None, :], 0), axis=-1) + rank[:, :MOE_TOPK]
    n_blocks = -(-n_assign // MOE_BLOCK) + MOE_EXPERTS
    blk_row0 = jnp.arange(n_blocks, dtype=jnp.int32) * MOE_BLOCK
    blk_expert = jnp.minimum(jnp.sum((pad_end[None, :] <= blk_row0[:, None]).astype(jnp.int32), axis=1),
                             MOE_EXPERTS - 1)
    nused = (pad_end[-1:] // MOE_BLOCK).astype(jnp.int32)
    pos_kt = dest.T.reshape(-1).astype(jnp.int32)
    seg = jnp.concatenate([pad_start + counts, pad_end]).astype(jnp.int32)
    return blk_expert, nused, pos_kt, seg


def _pick(n, pref):
    return pref if n % pref == 0 else n


def kernel(x, norm1_g, w_in, qk_norm_g, lambda_qk, attn_head_norm_g, rel_bias, conv_w, conv_b, dt_bias, a_log,
           d_skip, ssm_norm_g, w_attn_out, w_ssm_out, w_out, norm2_g, w_router_group, b_router_group,
           w_router_expert, b_router_expert, w_exp_gate, w_exp_up, w_exp_down):
    b, s, d = x.shape
    t = b * s
    tq = _pick(s, 256)
    bias_win = _bias_windows(rel_bias, s, tq)

    lane = np.arange(DT_COLS)
    grp, rem = lane // DT_LANES_PER_GROUP, lane % DT_LANES_PER_GROUP
    direction, e = rem // SSM_HEADS_PER_GROUP, rem % SSM_HEADS_PER_GROUP
    dt_src = direction * SSM_HEADS + grp * SSM_HEADS_PER_GROUP + e
    pad_lanes = LANES - DT_COLS

    xf = x.reshape(t, d)
    for i in range(DEPTH):
        lam_init = 0.8 - 0.6 * math.exp(-0.3 * i)
        g1 = norm1_g[i][None, :]
        w_tail = lax.slice(w_in, (i, 0, MAIN_COLS), (i + 1, d, w_in.shape[2]))
        w_dt = jnp.pad(w_tail[0, :, :DT_COLS][:, dt_src], ((0, 0), (0, pad_lanes))).astype(BF16)
        dt_b = jnp.pad(dt_bias[i].reshape(-1)[dt_src], (0, pad_lanes))[None, :]
        a_neg = jnp.pad(-jnp.exp(a_log[i].astype(F32)).reshape(-1)[dt_src], (0, pad_lanes))[None, :]

        proj = _norm_matmul(xf, g1, w_in, i, w_tail[0, :, DT_COLS:], BF16, _pick(t, 2048), 1024)
        proj3 = proj.reshape(b, s, PROJ_COLS)
        ac, act, dtt = _dt_prep(xf.reshape(b, s, d), g1, w_dt, dt_b, a_neg)

        u = _conv_silu(proj3, conv_w[i], conv_b[i][None, :], 256)
        y_ssd = _ssd_scan(u, ac, act, dtt, jnp.repeat(d_skip[i], SSM_HEAD_DIM)[None, :])

        gqk = jnp.tile(qk_norm_g[i], (1, 2))
        attn = _diff_attention(proj3, bias_win, rel_bias.T.astype(F32), gqk, lambda_qk[i],
                               attn_head_norm_g[i][None, :], tq, lam_init)

        w_r = jnp.pad(jnp.concatenate([w_router_group[i], w_router_expert[i]], axis=1),
                      ((0, 0), (0, LANES - MOE_GROUPS - MOE_EXPERTS)))
        b_r = jnp.pad(jnp.concatenate([b_router_group[i], b_router_expert[i]]),
                      (0, LANES - MOE_GROUPS - MOE_EXPERTS))[None, :]
        x_mid, hn, ids, gates = _post_mixer(
            attn.reshape(t, ATTN_WIDTH), y_ssd.reshape(t, SSM_INNER), proj, xf, ssm_norm_g[i][None, :],
            w_attn_out[i].astype(BF16), w_ssm_out[i].astype(BF16), w_out[i].astype(BF16), norm2_g[i][None, :],
            w_r, b_r, _pick(t, 256))

        blk_expert, nused, pos_kt, seg = _dispatch_plan(ids, _pick(t, 256))
        y_pad = _moe_experts(blk_expert, pos_kt, nused, seg, hn, w_exp_gate, w_exp_up, w_exp_down, i)
        xf = _moe_combine(pos_kt, x_mid, gates, y_pad, _pick(t, 128))
    return xf.reshape(b, s, d)
```

```python
import functools
import math

import jax
import jax.numpy as jnp
import numpy as np
from jax import lax
from jax.experimental import pallas as pl
from jax.experimental.pallas import tpu as pltpu

D_MODEL = 1024
DEPTH = 2
ATTN_HEADS = 8
ATTN_HEAD_DIM = 64
ATTN_V_DIM = 2 * ATTN_HEAD_DIM
ATTN_WIDTH = ATTN_HEADS * ATTN_V_DIM
NUM_BUCKETS = 32
MAX_DISTANCE = 128
SSM_INNER = 2 * D_MODEL
SSM_HEAD_DIM = 64
SSM_HEADS = SSM_INNER // SSM_HEAD_DIM
SSM_GROUPS = 4
SSM_HEADS_PER_GROUP = SSM_HEADS // SSM_GROUPS
SSM_STATE = 128
SSM_CONV = 5
SSM_CHUNK = 128
SSM_CONV_CH = SSM_INNER + 2 * SSM_GROUPS * SSM_STATE
MOE_GROUPS = 4
MOE_EXPERTS_PER_GROUP = 8
MOE_EXPERTS = MOE_GROUPS * MOE_EXPERTS_PER_GROUP
MOE_TOPK = 2
MOE_FF = 512
EXPERT_ROWS = 256
RMS_EPS = 1e-6

LANES = 128
COL_Q = 0
COL_K = COL_Q + ATTN_WIDTH
COL_V = COL_K + ATTN_WIDTH
COL_Z = COL_V + ATTN_WIDTH
COL_XBC = COL_Z + SSM_INNER
MAIN_COLS = COL_XBC + SSM_CONV_CH
DT_COLS = 2 * SSM_HEADS
COL_GATES = MAIN_COLS + DT_COLS
COL_GA = MAIN_COLS
COL_GM = COL_GA + D_MODEL
PROJ_COLS = COL_GM + D_MODEL
DT_LANES_PER_GROUP = 2 * SSM_HEADS_PER_GROUP
PACKED = D_MODEL // 2

VMEM_LIMIT_BYTES = 56 * 1024 * 1024
LOG2E = 1.4426950408889634

F32 = jnp.float32
BF16 = jnp.bfloat16


def _params(*semantics):
    return pltpu.CompilerParams(dimension_semantics=semantics, vmem_limit_bytes=VMEM_LIMIT_BYTES)


def _sigmoid(x):
    return 0.5 * jnp.tanh(0.5 * x) + 0.5


def _rms(x, g):
    ms = jnp.mean(x * x, axis=-1, keepdims=True)
    return x * lax.rsqrt(ms + RMS_EPS) * g


def _norm_matmul_kernel(x_ref, g_ref, w_ref, wg_ref, o_ref, hn_ref, *, n_main):
    j = pl.program_id(1)

    @pl.when(j == 0)
    def _():
        hn_ref[...] = _rms(x_ref[...], g_ref[...]).astype(hn_ref.dtype)

    @pl.when(j < n_main)
    def _():
        o_ref[...] = jnp.dot(hn_ref[...], w_ref[...].astype(BF16), preferred_element_type=F32).astype(o_ref.dtype)

    @pl.when(j >= n_main)
    def _():
        o_ref[...] = jnp.dot(hn_ref[...], wg_ref[...].astype(BF16), preferred_element_type=F32).astype(o_ref.dtype)


def _norm_matmul(x, g, w, layer, w_gates, out_dtype, tm, tn):
    t, d = x.shape
    n_main = MAIN_COLS // tn
    n_gate = w_gates.shape[1] // tn
    return pl.pallas_call(
        functools.partial(_norm_matmul_kernel, n_main=n_main),
        grid=(t // tm, n_main + n_gate),
        in_specs=[
            pl.BlockSpec((tm, d), lambda i, j: (i, 0)),
            pl.BlockSpec((1, d), lambda i, j: (0, 0)),
            pl.BlockSpec((None, d, tn), lambda i, j: (layer, 0, jnp.minimum(j, n_main - 1))),
            pl.BlockSpec((d, tn), lambda i, j: (0, jnp.maximum(j - n_main, 0))),
        ],
        out_specs=pl.BlockSpec((tm, tn), lambda i, j: (i, j)),
        out_shape=jax.ShapeDtypeStruct((t, (n_main + n_gate) * tn), out_dtype),
        scratch_shapes=[pltpu.VMEM((tm, d), BF16)],
        compiler_params=_params("parallel", "arbitrary"),
        name="in_proj",
    )(x, g, w, w_gates)


def _dt_prep_kernel(x_ref, g_ref, w_ref, bias_ref, a_ref, ac_ref, act_ref, dtt_ref):
    hn = _rms(x_ref[...], g_ref[...]).astype(BF16)
    raw = jnp.dot(hn, w_ref[...], preferred_element_type=F32) + bias_ref[...]
    dt = jnp.maximum(raw, 0.0) + jnp.log1p(jnp.exp(-jnp.abs(raw)))
    a = dt * (a_ref[...] * LOG2E)
    s = a.shape[0]
    pos = lax.broadcasted_iota(jnp.int32, a.shape, 0) & (SSM_CHUNK - 1)
    pre = a
    suf = a
    k = 1
    while k < SSM_CHUNK:
        pre = pre + jnp.where(pos >= k, pltpu.roll(pre, k, 0), 0.0)
        suf = suf + jnp.where(pos < SSM_CHUNK - k, pltpu.roll(suf, s - k, 0), 0.0)
        k *= 2
    lane = lax.broadcasted_iota(jnp.int32, a.shape, 1)
    is_fwd = (lane & (DT_LANES_PER_GROUP - 1)) < SSM_HEADS_PER_GROUP
    ac = jnp.where(is_fwd, pre, suf)
    act_ref[...] = ac.T
    dtt_ref[...] = dt.T
    for grp in range(SSM_GROUPS):
        shift = (LANES - grp * DT_LANES_PER_GROUP) % LANES
        ac_ref[:, grp * LANES:(grp + 1) * LANES] = ac if shift == 0 else pltpu.roll(ac, shift, 1)


def _dt_prep(x3, g, w_dt, bias, a_neg):
    b, s, d = x3.shape
    out = jax.ShapeDtypeStruct((b, s, SSM_GROUPS * LANES), F32)
    out_t = jax.ShapeDtypeStruct((b, LANES, s), F32)
    vec = pl.BlockSpec((1, LANES), lambda i: (0, 0))
    return pl.pallas_call(
        _dt_prep_kernel,
        grid=(b,),
        in_specs=[
            pl.BlockSpec((None, s, d), lambda i: (i, 0, 0)),
            pl.BlockSpec((1, d), lambda i: (0, 0)),
            pl.BlockSpec((d, LANES), lambda i: (0, 0)),
            vec,
            vec,
        ],
        out_specs=[
            pl.BlockSpec((None, s, SSM_GROUPS * LANES), lambda i: (i, 0, 0)),
            pl.BlockSpec((None, LANES, s), lambda i: (i, 0, 0)),
            pl.BlockSpec((None, LANES, s), lambda i: (i, 0, 0)),
        ],
        out_shape=[out, out_t, out_t],
        compiler_params=_params("parallel"),
        name="dt_prep",
    )(x3, g, w_dt, bias, a_neg)


def _conv_kernel(x_ref, w_ref, b_ref, o_ref):
    xf = x_ref[...].astype(F32)
    s, tc = xf.shape
    pad = SSM_CONV // 2
    xe = jnp.concatenate([xf, jnp.zeros((8, tc), F32)], axis=0)
    acc = xf * w_ref[pad:pad + 1, :] + b_ref[...]
    for j in range(SSM_CONV):
        shift = pad - j
        if shift != 0:
            acc = acc + pltpu.roll(xe, shift % (s + 8), 0)[:s, :] * w_ref[j:j + 1, :]
    o_ref[...] = (acc * _sigmoid(acc)).astype(o_ref.dtype)


def _conv_silu(proj3, conv_w, conv_b, tc):
    b, s, _ = proj3.shape
    col0 = COL_XBC // tc
    return pl.pallas_call(
        _conv_kernel,
        grid=(b, SSM_CONV_CH // tc),
        in_specs=[
            pl.BlockSpec((None, s, tc), lambda i, c: (i, 0, col0 + c)),
            pl.BlockSpec((SSM_CONV, tc), lambda i, c: (0, c)),
            pl.BlockSpec((1, tc), lambda i, c: (0, c)),
        ],
        out_specs=pl.BlockSpec((None, s, tc), lambda i, c: (i, 0, c)),
        out_shape=jax.ShapeDtypeStruct((b, s, SSM_CONV_CH), BF16),
        compiler_params=_params("parallel", "parallel"),
        name="conv_silu",
    )(proj3, conv_w, conv_b)


def _rel_bucket(rel):
    half = NUM_BUCKETS // 2
    max_exact = half // 2
    ret = jnp.where(rel > 0, half, 0)
    n = jnp.abs(rel)
    n_safe = jnp.maximum(n, 1).astype(F32)
    large = max_exact + (jnp.log(n_safe / max_exact) / math.log(MAX_DISTANCE / max_exact)
                         * (half - max_exact)).astype(jnp.int32)
    large = jnp.minimum(large, half - 1)
    return ret + jnp.where(n < max_exact, n, large)


def _bias_kernel(relb_ref, bid_ref, o_ref, *, tq):
    h = pl.program_id(0)
    bid = bid_ref[...]
    fv = jnp.zeros(bid.shape, F32)
    for bkt in range(NUM_BUCKETS):
        fv = jnp.where(bid == bkt, relb_ref[h, bkt], fv)
    g = jnp.broadcast_to(fv, o_ref.shape)
    row = lax.broadcasted_iota(jnp.int32, o_ref.shape, 0)
    k = 1
    while k < tq:
        g = jnp.where((row & k) != 0, pltpu.roll(g, k, 1), g)
        k *= 2
    o_ref[...] = g


def _bias_windows(rel_bias, s, tq):
    width = 2 * s
    m = jnp.arange(width, dtype=jnp.int32)
    bid = _rel_bucket(((m + tq) % width) - s).astype(jnp.int32)[None, :]
    return pl.pallas_call(
        functools.partial(_bias_kernel, tq=tq),
        grid=(ATTN_HEADS,),
        in_specs=[
            pl.BlockSpec(memory_space=pltpu.SMEM),
            pl.BlockSpec((1, width), lambda h: (0, 0)),
        ],
        out_specs=pl.BlockSpec((None, tq, width), lambda h: (h, 0, 0)),
        out_shape=jax.ShapeDtypeStruct((ATTN_HEADS, tq, width), F32),
        compiler_params=_params("parallel"),
        name="rel_bias_windows",
    )(rel_bias.T.astype(F32), bid)


SOFTMAX_SHIFT_SLACK = 40.0
BF16_ROUNDING_SLACK = 1.01


def _attn_kernel(relb_ref, q_ref, k_ref, v_ref, bias_ref, gqk_ref, lamqk_ref, gh_ref, o_ref, q0_ref, q1_ref, kn_ref,
                 vx_ref, scal_ref, *, tq, lam_init):
    h = pl.program_id(0)
    qi = pl.program_id(2)
    s = k_ref.shape[0]
    lo = lax.broadcasted_iota(jnp.int32, (1, LANES), 1) < ATTN_HEAD_DIM

    def half_sums(x2):
        s_all = jnp.sum(x2, axis=-1, keepdims=True)
        s_lo = jnp.sum(jnp.where(lo, x2, 0.0), axis=-1, keepdims=True)
        return s_lo, jnp.maximum(s_all - s_lo, 0.0)

    def half_norm(x, g):
        s_lo, s_hi = half_sums(x * x)
        ms = jnp.where(lo, s_lo, s_hi) * (1.0 / ATTN_HEAD_DIM)
        return x * lax.rsqrt(ms + RMS_EPS) * g

    @pl.when(qi == 0)
    def _():
        kn_ref[...] = half_norm(k_ref[...].astype(F32), gqk_ref[1:2, :]).astype(BF16)
        vx_ref[:, :LANES] = v_ref[...]
        vx_ref[:, LANES:] = jnp.ones((s, LANES), BF16)
        bmax = relb_ref[h, 0]
        bmin = relb_ref[h, 0]
        for bkt in range(1, NUM_BUCKETS):
            bmax = jnp.maximum(bmax, relb_ref[h, bkt])
            bmin = jnp.minimum(bmin, relb_ref[h, bkt])
        qn = (half_norm(q_ref[...].astype(F32), gqk_ref[0:1, :]) * (ATTN_HEAD_DIM ** -0.5)).astype(BF16)
        q0_ref[...] = jnp.where(lo, qn, jnp.zeros_like(qn))
        q1_ref[...] = jnp.where(lo, jnp.zeros_like(qn), qn)
        gains = jnp.abs(gqk_ref[...])
        reach = (BF16_ROUNDING_SLACK * ATTN_HEAD_DIM ** 0.5) * jnp.max(gains[0:1, :]) * jnp.max(gains[1:2, :])
        scal_ref[0] = reach + bmax
        scal_ref[1] = 2.0 * reach + (bmax - bmin)

    rows = pl.ds(pl.multiple_of(qi * tq, tq), tq)
    q0 = q0_ref[rows, :]
    q1 = q1_ref[rows, :]
    shift = scal_ref[0]
    slack = scal_ref[1]
    c0 = pl.multiple_of(s - (qi + 1) * tq, tq)
    lq = lamqk_ref[...]
    lam = (jnp.exp(jnp.sum(lq[0:1] * lq[1:2], axis=-1, keepdims=True))
           - jnp.exp(jnp.sum(lq[2:3] * lq[3:4], axis=-1, keepdims=True)) + lam_init)

    def logits_of(qm):
        return lax.dot_general(qm, kn_ref[...], (((1,), (1,)), ((), ())),
                               preferred_element_type=F32) + bias_ref[:, pl.ds(c0, s)]

    def finish(out):
        o_ref[...] = (_rms(out, gh_ref[...]) * (1.0 - lam_init)).astype(o_ref.dtype)

    @pl.when(slack <= SOFTMAX_SHIFT_SLACK)
    def _():
        def weighted_v(qm):
            p = jnp.exp(logits_of(qm) - shift).astype(BF16)
            return jnp.dot(p, vx_ref[...], preferred_element_type=F32)

        o0 = weighted_v(q0)
        o1 = weighted_v(q1)
        finish(o0[:, :LANES] / o0[:, LANES:LANES + 1] - o1[:, :LANES] * (lam / o1[:, LANES:LANES + 1]))

    @pl.when(slack > SOFTMAX_SHIFT_SLACK)
    def _():
        def softmax_map(qm):
            logits = logits_of(qm)
            p = jnp.exp(logits - jnp.max(logits, axis=-1, keepdims=True))
            return p, jnp.sum(p, axis=-1, keepdims=True)

        p0, l0 = softmax_map(q0)
        p1, l1 = softmax_map(q1)
        attn = p0 * (1.0 / l0) - p1 * (lam / l1)
        finish(jnp.dot(attn.astype(BF16), v_ref[...], preferred_element_type=F32))


def _diff_attention(proj3, bias_win, relb_t, gqk, lamqk, gh, tq, lam_init):
    b, s, _ = proj3.shape
    width = bias_win.shape[-1]
    return pl.pallas_call(
        functools.partial(_attn_kernel, tq=tq, lam_init=lam_init),
        grid=(ATTN_HEADS, b, s // tq),
        in_specs=[
            pl.BlockSpec(memory_space=pltpu.SMEM),
            pl.BlockSpec((None, s, LANES), lambda h, i, q: (i, 0, COL_Q // LANES + h)),
            pl.BlockSpec((None, s, LANES), lambda h, i, q: (i, 0, COL_K // LANES + h)),
            pl.BlockSpec((None, s, LANES), lambda h, i, q: (i, 0, COL_V // LANES + h)),
            pl.BlockSpec((None, tq, width), lambda h, i, q: (h, 0, 0)),
            pl.BlockSpec((2, LANES), lambda h, i, q: (0, 0)),
            pl.BlockSpec((4, ATTN_HEAD_DIM), lambda h, i, q: (0, 0)),
            pl.BlockSpec((1, LANES), lambda h, i, q: (0, 0)),
        ],
        out_specs=pl.BlockSpec((None, tq, LANES), lambda h, i, q: (i, q, h)),
        out_shape=jax.ShapeDtypeStruct((b, s, ATTN_WIDTH), BF16),
        scratch_shapes=[pltpu.VMEM((s, LANES), BF16), pltpu.VMEM((s, LANES), BF16), pltpu.VMEM((s, LANES), BF16),
                        pltpu.VMEM((s, 2 * LANES), BF16), pltpu.SMEM((2,), F32)],
        compiler_params=_params("parallel", "parallel", "arbitrary"),
        name="diff_attention",
    )(relb_t, proj3, proj3, proj3, bias_win, gqk, lamqk, gh)


def _ssd_kernel(x_ref, b_ref, c_ref, ac_ref, act_ref, dtt_ref, dsk_ref, y_ref, st_ref, acc_ref, bt_ref):
    L = SSM_CHUNK
    s = x_ref.shape[0]
    n_chunks = s // L
    row = lax.broadcasted_iota(jnp.int32, (L, L), 0)
    col = lax.broadcasted_iota(jnp.int32, (L, L), 1)
    lo = lax.broadcasted_iota(jnp.int32, (L, LANES), 1) < SSM_HEAD_DIM
    lo1 = lo[0:1, :]
    n_pairs = SSM_HEADS_PER_GROUP // 2

    bt_ref[...] = b_ref[...].astype(F32).T.astype(BF16)

    def one_chunk(c, backward):
        d = 1 if backward else 0
        rows = pl.ds(pl.multiple_of(c * L, L), L)
        off = SSM_HEADS_PER_GROUP * d
        tot_row = 0 if backward else L - 1
        mask = (row <= col) if backward else (row >= col)
        xb = x_ref[rows, :]
        cc = c_ref[rows, :]
        acc = ac_ref[rows, :]
        act = act_ref[:, rows]
        dtt = dtt_ref[:, rows]
        bt = bt_ref[:, rows].astype(F32)
        cb = lax.dot_general(cc, b_ref[rows, :], (((1,), (1,)), ((), ())), preferred_element_type=F32)

        ys = []
        for j in range(n_pairs):
            lanes = slice(j * LANES, (j + 1) * LANES)
            x_pair = xb[:, lanes]
            y_off = jnp.dot(cc, st_ref[d, :, lanes].astype(BF16), preferred_element_type=F32)
            diag, stn, acol, tot = [], [], [], []
            for e in (2 * j, 2 * j + 1):
                a_col = jnp.broadcast_to(acc[:, off + e:off + e + 1], (L, LANES))
                a_row = act[off + e:off + e + 1, :]
                dt_row = dtt[off + e:off + e + 1, :]
                tot_e = acc[tot_row:tot_row + 1, off + e:off + e + 1]
                decay = jnp.where(mask, jnp.exp2(a_col - a_row), 0.0)
                diag.append(jnp.dot((cb * decay * dt_row).astype(BF16), x_pair, preferred_element_type=F32))
                w_row = dt_row * jnp.exp2(tot_e - a_row)
                stn.append(jnp.dot((bt * w_row).astype(BF16), x_pair, preferred_element_type=F32))
                acol.append(a_col)
                tot.append(tot_e)
            ys.append(jnp.where(lo, diag[0], diag[1]) + y_off * jnp.exp2(jnp.where(lo, acol[0], acol[1])))
            st_ref[d, :, lanes] = (st_ref[d, :, lanes] * jnp.exp2(jnp.where(lo1, tot[0], tot[1]))
                                   + jnp.where(lo, stn[0], stn[1]))
        y = jnp.concatenate(ys, axis=1)
        if not backward:
            y = y + xb.astype(F32) * dsk_ref[...]
        acc_ref[d, rows, :] = y

    st_ref[...] = jnp.zeros_like(st_ref)

    def body(c, carry):
        one_chunk(c, False)
        one_chunk(n_chunks - 1 - c, True)
        return carry

    lax.fori_loop(0, n_chunks, body, 0)
    y_ref[...] = (acc_ref[0] + acc_ref[1]).astype(y_ref.dtype)


def _ssd_scan(u, ac_g, act_g, dtt_g, dskip):
    b, s, _ = u.shape
    gw = SSM_HEADS_PER_GROUP * SSM_HEAD_DIM
    xcols = SSM_INNER // SSM_STATE
    nl = DT_LANES_PER_GROUP
    return pl.pallas_call(
        _ssd_kernel,
        grid=(b, SSM_GROUPS),
        in_specs=[
            pl.BlockSpec((None, s, gw), lambda i, g: (i, 0, g)),
            pl.BlockSpec((None, s, SSM_STATE), lambda i, g: (i, 0, xcols + g)),
            pl.BlockSpec((None, s, SSM_STATE), lambda i, g: (i, 0, xcols + SSM_GROUPS + g)),
            pl.BlockSpec((None, s, LANES), lambda i, g: (i, 0, g)),
            pl.BlockSpec((None, nl, s), lambda i, g: (i, g, 0)),
            pl.BlockSpec((None, nl, s), lambda i, g: (i, g, 0)),
            pl.BlockSpec((1, gw), lambda i, g: (0, g)),
        ],
        out_specs=pl.BlockSpec((None, s, gw), lambda i, g: (i, 0, g)),
        out_shape=jax.ShapeDtypeStruct((b, s, SSM_INNER), BF16),
        scratch_shapes=[pltpu.VMEM((2, SSM_STATE, gw), F32), pltpu.VMEM((2, s, gw), F32),
                        pltpu.VMEM((SSM_STATE, s), BF16)],
        compiler_params=_params("parallel", "parallel"),
        name="ssd_scan",
    )(u, u, u, ac_g, act_g, dtt_g, dskip)


def _route(logits):
    lane = lax.broadcasted_iota(jnp.int32, logits.shape, 1)
    lane_f = lane.astype(F32)
    neg = -jnp.inf
    big = float(LANES)
    is_g = lane < MOE_GROUPS
    gl = jnp.where(is_g, logits, neg)
    gmax = jnp.max(gl, axis=-1, keepdims=True)
    g_idx = jnp.min(jnp.where(gl == gmax, lane_f, big), axis=-1, keepdims=True)
    g_gate = 1.0 / jnp.sum(jnp.where(is_g, jnp.exp(logits - gmax), 0.0), axis=-1, keepdims=True)
    first = MOE_GROUPS + MOE_EXPERTS_PER_GROUP * g_idx
    in_grp = (lane_f >= first) & (lane_f < first + MOE_EXPERTS_PER_GROUP)
    el = jnp.where(in_grp, logits, neg)
    v1 = jnp.max(el, axis=-1, keepdims=True)
    i1 = jnp.min(jnp.where(el == v1, lane_f, big), axis=-1, keepdims=True)
    el2 = jnp.where(lane_f == i1, neg, el)
    v2 = jnp.max(el2, axis=-1, keepdims=True)
    i2 = jnp.min(jnp.where(el2 == v2, lane_f, big), axis=-1, keepdims=True)
    r = jnp.exp(v2 - v1)
    p1 = 1.0 / (1.0 + r)
    p2 = r * p1
    ids = jnp.where(lane == 0, i1 - MOE_GROUPS, jnp.where(lane == 1, i2 - MOE_GROUPS, 0.0)).astype(jnp.int32)
    gates = jnp.where(lane == 0, p1 * g_gate, jnp.where(lane == 1, p2 * g_gate, 0.0))
    return ids, gates


_HI_HALF = 0xFFFF0000


def _pack_bf16_pairs(x):
    n = x.shape[1] // 2
    bits = lax.bitcast_convert_type(x.astype(BF16).astype(F32), jnp.uint32)
    return (bits[:, n:] & jnp.uint32(_HI_HALF)) | (bits[:, :n] >> 16)


def _unpack_bf16_pairs(w):
    lo = lax.bitcast_convert_type(w << 16, F32)
    hi = lax.bitcast_convert_type(w & jnp.uint32(_HI_HALF), F32)
    return jnp.concatenate([lo, hi], axis=1)


def _post_kernel(attn_ref, y_ref, z0_ref, z1_ref, ga_ref, gm_ref, x_ref, gs_ref, wa_ref, ws_ref, wo_ref, g2_ref,
                 wr_ref, br_ref, xo_ref, hn_ref, id_ref, gate_ref):
    ya = jnp.dot(attn_ref[...], wa_ref[...], preferred_element_type=F32)
    z = jnp.concatenate([z0_ref[...], z1_ref[...]], axis=1).astype(F32)
    gated = y_ref[...].astype(F32) * (z * _sigmoid(z))
    ym = jnp.dot(_rms(gated, gs_ref[...]).astype(BF16), ws_ref[...], preferred_element_type=F32)
    merged = (_sigmoid(ga_ref[...].astype(F32)) * ya + _sigmoid(gm_ref[...].astype(F32)) * ym)
    x_new = x_ref[...] + jnp.dot(merged.astype(BF16), wo_ref[...], preferred_element_type=F32)
    xo_ref[...] = x_new
    hn = _rms(x_new, g2_ref[...])
    hn_ref[:, 0, :] = _pack_bf16_pairs(hn)
    hn_hi = hn.astype(BF16)
    hn_lo = (hn - hn_hi.astype(F32)).astype(BF16)
    wr = wr_ref[...]
    wr_hi = wr.astype(BF16)
    wr_lo = (wr - wr_hi.astype(F32)).astype(BF16)
    logits = (jnp.dot(hn_hi, wr_hi, preferred_element_type=F32) + jnp.dot(hn_lo, wr_hi, preferred_element_type=F32)
              + jnp.dot(hn_hi, wr_lo, preferred_element_type=F32) + br_ref[...])
    ids, gates = _route(logits)
    id_ref[...] = ids
    gate_ref[...] = gates


def _post_mixer(attn, y, proj, x, gs, wa, ws, wo, g2, wr, br, tm):
    t, d = x.shape
    row = lambda w, c: pl.BlockSpec((tm, w), lambda i: (i, c))
    full = lambda a: pl.BlockSpec(a.shape, lambda i: (0,) * a.ndim)
    return pl.pallas_call(
        _post_kernel,
        grid=(t // tm,),
        in_specs=[
            row(ATTN_WIDTH, 0),
            row(SSM_INNER, 0),
            row(D_MODEL, COL_Z // D_MODEL),
            row(D_MODEL, COL_Z // D_MODEL + 1),
            row(D_MODEL, COL_GA // D_MODEL),
            row(D_MODEL, COL_GM // D_MODEL),
            row(D_MODEL, 0),
            full(gs), full(wa), full(ws), full(wo), full(g2), full(wr), full(br),
        ],
        out_specs=[row(D_MODEL, 0), pl.BlockSpec((tm, 1, PACKED), lambda i: (i, 0, 0)), row(LANES, 0), row(LANES, 0)],
        out_shape=[
            jax.ShapeDtypeStruct((t, d), F32),
            jax.ShapeDtypeStruct((t, 1, PACKED), jnp.uint32),
            jax.ShapeDtypeStruct((t, LANES), jnp.int32),
            jax.ShapeDtypeStruct((t, LANES), F32),
        ],
        compiler_params=_params("parallel"),
        name="post_mixer",
    )(attn, y, proj, proj, proj, proj, x, gs, wa, ws, wo, g2, wr, br)


GATHER_DEPTH = 3


def _start_rows(idx_ref, base, src_hbm, dst_ref, sem, rows=None):
    for r in (range(dst_ref.shape[0]) if rows is None else rows):
        pltpu.make_async_copy(src_hbm.at[idx_ref[base + r]], dst_ref.at[pl.ds(r, 1), :], sem).start(priority=r % 2)


def _wait_buffer(dst_ref, sem):
    pltpu.make_async_copy(dst_ref, dst_ref, sem).wait()


def _moe_kernel(blk_e_ref, pos_ref, nused_ref, seg_ref, hn_hbm, wg_ref, wu_ref, wd_ref, y_ref, src_ref, xg_ref,
                wgb_ref, wub_ref, wdb_ref, sem, *, n_tokens):
    i = pl.program_id(0)
    nused = nused_ref[0]
    quarter = EXPERT_ROWS // 4

    def start(blk, part=None):
        slot = blk % GATHER_DEPTH
        rows = range(EXPERT_ROWS) if part is None else range(part * quarter, (part + 1) * quarter)
        _start_rows(src_ref, blk * EXPERT_ROWS, hn_hbm, xg_ref.at[slot], sem.at[slot], rows)

    @pl.when(i == 0)
    def _():
        def clear(j, carry):
            src_ref[j] = 0
            return carry

        for e in range(MOE_EXPERTS):
            lax.fori_loop(seg_ref[e], seg_ref[MOE_EXPERTS + e], clear, 0)
        for k in range(MOE_TOPK):
            def fill(tok, carry):
                src_ref[pos_ref[k * n_tokens + tok]] = tok
                return carry

            lax.fori_loop(0, n_tokens, fill, 0, unroll=8)
        start(0)

    @pl.when((i == 0) & (nused > 1))
    def _():
        start(1)

    new_expert = (i == 0) | (blk_e_ref[i] != blk_e_ref[jnp.maximum(i - 1, 0)])

    @pl.when((i < nused) & new_expert)
    def _():
        wgb_ref[...] = wg_ref[...].astype(BF16)
        wub_ref[...] = wu_ref[...].astype(BF16)
        wdb_ref[...] = wd_ref[...].astype(BF16)

    def compute(prefetch):
        xb = _unpack_bf16_pairs(xg_ref[i % GATHER_DEPTH]).astype(BF16)
        prefetch(0)
        hg = jnp.dot(xb, wgb_ref[...], preferred_element_type=F32)
        prefetch(1)
        hu = jnp.dot(xb, wub_ref[...], preferred_element_type=F32)
        prefetch(2)
        hid = (hg * _sigmoid(hg) * hu).astype(BF16)
        y = jnp.dot(hid, wdb_ref[...], preferred_element_type=F32)
        prefetch(3)
        y_ref[:, 0, :] = _pack_bf16_pairs(y)

    def wait():
        slot = i % GATHER_DEPTH
        _wait_buffer(xg_ref.at[slot], sem.at[slot])

    @pl.when(i + 2 < nused)
    def _():
        wait()
        compute(lambda part: start(i + 2, part))

    @pl.when((i < nused) & (i + 2 >= nused))
    def _():
        wait()
        compute(lambda part: None)

    @pl.when(i >= nused)
    def _():
        y_ref[...] = jnp.zeros_like(y_ref)


def _moe_experts(blk_expert, pos_kt, nused, seg, hn3, wg, wu, wd, layer):
    n_blocks = blk_expert.shape[0]
    d = wg.shape[2]
    n_tokens = hn3.shape[0]
    grid_spec = pltpu.PrefetchScalarGridSpec(
        num_scalar_prefetch=4,
        grid=(n_blocks,),
        in_specs=[
            pl.BlockSpec(memory_space=pl.ANY),
            pl.BlockSpec((None, None, d, MOE_FF), lambda i, be, st, nu, sg: (layer, be[i], 0, 0)),
            pl.BlockSpec((None, None, d, MOE_FF), lambda i, be, st, nu, sg: (layer, be[i], 0, 0)),
            pl.BlockSpec((None, None, MOE_FF, d), lambda i, be, st, nu, sg: (layer, be[i], 0, 0)),
        ],
        out_specs=pl.BlockSpec((EXPERT_ROWS, 1, PACKED), lambda i, be, st, nu, sg: (i, 0, 0)),
        scratch_shapes=[
            pltpu.SMEM((n_blocks * EXPERT_ROWS,), jnp.int32),
            pltpu.VMEM((GATHER_DEPTH, EXPERT_ROWS, PACKED), jnp.uint32),
            pltpu.VMEM((d, MOE_FF), BF16),
            pltpu.VMEM((d, MOE_FF), BF16),
            pltpu.VMEM((MOE_FF, d), BF16),
            pltpu.SemaphoreType.DMA((GATHER_DEPTH,)),
        ],
    )
    return pl.pallas_call(
        functools.partial(_moe_kernel, n_tokens=n_tokens),
        grid_spec=grid_spec,
        out_shape=jax.ShapeDtypeStruct((n_blocks * EXPERT_ROWS, 1, PACKED), jnp.uint32),
        compiler_params=_params("arbitrary"),
        name="moe_experts",
    )(blk_expert, pos_kt, nused, seg, hn3, wg, wu, wd)


def _combine_kernel(pos_ref, x_ref, gate_ref, y_hbm, o_ref, yg_ref, sem, *, tt, n_tiles):
    i = pl.program_id(0)

    def start(tile):
        slot = tile % GATHER_DEPTH
        for k in range(MOE_TOPK):
            _start_rows(pos_ref, (k * n_tiles + tile) * tt, y_hbm, yg_ref.at[slot, k], sem.at[slot])

    def finish():
        gates = gate_ref[...]
        slot = i % GATHER_DEPTH
        o_ref[...] = (x_ref[...] + gates[:, 0:1] * _unpack_bf16_pairs(yg_ref[slot, 0])
                      + gates[:, 1:2] * _unpack_bf16_pairs(yg_ref[slot, 1]))

    def wait():
        slot = i % GATHER_DEPTH
        _wait_buffer(yg_ref.at[slot], sem.at[slot])

    @pl.when(i == 0)
    def _():
        start(0)
        if n_tiles > 1:
            start(1)

    @pl.when(i + 2 < n_tiles)
    def _():
        wait()
        start(i + 2)
        finish()

    @pl.when(i + 2 >= n_tiles)
    def _():
        wait()
        finish()


def _moe_combine(pos_kt, x, gates, y3, tt):
    t, d = x.shape
    n_tiles = t // tt
    grid_spec = pltpu.PrefetchScalarGridSpec(
        num_scalar_prefetch=1,
        grid=(n_tiles,),
        in_specs=[
            pl.BlockSpec((tt, d), lambda i, p: (i, 0)),
            pl.BlockSpec((tt, LANES), lambda i, p: (i, 0)),
            pl.BlockSpec(memory_space=pl.ANY),
        ],
        out_specs=pl.BlockSpec((tt, d), lambda i, p: (i, 0)),
        scratch_shapes=[pltpu.VMEM((GATHER_DEPTH, MOE_TOPK, tt, PACKED), jnp.uint32),
                        pltpu.SemaphoreType.DMA((GATHER_DEPTH,))],
    )
    return pl.pallas_call(
        functools.partial(_combine_kernel, tt=tt, n_tiles=n_tiles),
        grid_spec=grid_spec,
        out_shape=jax.ShapeDtypeStruct((t, d), F32),
        compiler_params=_params("arbitrary"),
        name="moe_combine",
    )(pos_kt, x, gates, y3)


def _plan_kernel(ids_ref, rank_ref, cnt_ref, carry_ref):
    @pl.when(pl.program_id(0) == 0)
    def _():
        carry_ref[...] = jnp.zeros_like(carry_ref)

    ids = ids_ref[...]
    tt = ids.shape[0]
    lane = lax.broadcasted_iota(jnp.int32, ids.shape, 1)
    oh0 = jnp.where(lane == ids[:, 0:1], 1.0, 0.0)
    oh1 = jnp.where(lane == ids[:, 1:2], 1.0, 0.0)
    earlier = jnp.where(lax.broadcasted_iota(jnp.int32, (tt, tt), 1) < lax.broadcasted_iota(jnp.int32, (tt, tt), 0),
                        1.0, 0.0).astype(BF16)
    before0 = jnp.dot(earlier, oh0.astype(BF16), preferred_element_type=F32)
    before1 = jnp.dot(earlier, oh1.astype(BF16), preferred_element_type=F32)
    tot0 = jnp.sum(oh0, axis=0, keepdims=True)
    carry = carry_ref[...]
    rank0 = jnp.sum(oh0 * (before0 + carry), axis=-1, keepdims=True)
    rank1 = jnp.sum(oh1 * (before1 + tot0 + carry), axis=-1, keepdims=True)
    carry = carry + tot0 + jnp.sum(oh1, axis=0, keepdims=True)
    carry_ref[...] = carry
    cnt_ref[...] = carry
    rank_ref[...] = jnp.where(lane == 0, rank0, jnp.where(lane == 1, rank1, 0.0)).astype(jnp.int32)


def _dispatch_plan(ids, tt):
    t = ids.shape[0]
    n_assign = t * MOE_TOPK
    rank, cnt = pl.pallas_call(
        _plan_kernel,
        grid=(t // tt,),
        in_specs=[pl.BlockSpec((tt, LANES), lambda i: (i, 0))],
        out_specs=[pl.BlockSpec((tt, LANES), lambda i: (i, 0)), pl.BlockSpec((1, LANES), lambda i: (0, 0))],
        out_shape=[jax.ShapeDtypeStruct((t, LANES), jnp.int32), jax.ShapeDtypeStruct((1, LANES), F32)],
        scratch_shapes=[pltpu.VMEM((1, LANES), F32)],
        compiler_params=_params("arbitrary"),
        name="moe_plan",
    )(ids)
    counts = cnt[0, :MOE_EXPERTS].astype(jnp.int32)
    padded = (counts + EXPERT_ROWS - 1) // EXPERT_ROWS * EXPERT_ROWS
    pad_end = jnp.cumsum(padded)
    pad_start = pad_end - padded
    e_ids = ids[:, :MOE_TOPK]
    onehot = e_ids[:, :, None] == jnp.arange(MOE_EXPERTS, dtype=jnp.int32)[None, None, :]
    dest = jnp.sum(jnp.where(onehot, pad_start[None, None, :], 0), axis=-1) + rank[:, :MOE_TOPK]
    n_blocks = -(-n_assign // EXPERT_ROWS) + MOE_EXPERTS
    blk_row0 = jnp.arange(n_blocks, dtype=jnp.int32) * EXPERT_ROWS
    blk_expert = jnp.minimum(jnp.sum((pad_end[None, :] <= blk_row0[:, None]).astype(jnp.int32), axis=1),
                             MOE_EXPERTS - 1)
    nused = (pad_end[-1:] // EXPERT_ROWS).astype(jnp.int32)
    pos_kt = dest.T.reshape(-1).astype(jnp.int32)
    seg = jnp.concatenate([pad_start + counts, pad_end]).astype(jnp.int32)
    return blk_expert, nused, pos_kt, seg


def _pick(n, pref):
    return pref if n % pref == 0 else n


def kernel(x, norm1_g, w_in, qk_norm_g, lambda_qk, attn_head_norm_g, rel_bias, conv_w, conv_b, dt_bias, a_log,
           d_skip, ssm_norm_g, w_attn_out, w_ssm_out, w_out, norm2_g, w_router_group, b_router_group,
           w_router_expert, b_router_expert, w_exp_gate, w_exp_up, w_exp_down):
    b, s, d = x.shape
    t = b * s
    tq = _pick(s, 512)
    bias_win = _bias_windows(rel_bias, s, tq)

    lane = np.arange(DT_COLS)
    grp, rem = lane // DT_LANES_PER_GROUP, lane % DT_LANES_PER_GROUP
    direction, e = rem // SSM_HEADS_PER_GROUP, rem % SSM_HEADS_PER_GROUP
    dt_src = direction * SSM_HEADS + grp * SSM_HEADS_PER_GROUP + e
    pad_lanes = LANES - DT_COLS

    xf = x.reshape(t, d)
    for i in range(DEPTH):
        lam_init = 0.8 - 0.6 * math.exp(-0.3 * i)
        g1 = norm1_g[i][None, :]
        w_tail = lax.slice(w_in, (i, 0, MAIN_COLS), (i + 1, d, w_in.shape[2]))
        w_dt = jnp.pad(w_tail[0, :, :DT_COLS][:, dt_src], ((0, 0), (0, pad_lanes))).astype(BF16)
        dt_b = jnp.pad(dt_bias[i].reshape(-1)[dt_src], (0, pad_lanes))[None, :]
        a_neg = jnp.pad(-jnp.exp(a_log[i].astype(F32)).reshape(-1)[dt_src], (0, pad_lanes))[None, :]

        proj = _norm_matmul(xf, g1, w_in, i, w_tail[0, :, DT_COLS:], BF16, _pick(t, 2048), 1024)
        proj3 = proj.reshape(b, s, PROJ_COLS)
        ac, act, dtt = _dt_prep(xf.reshape(b, s, d), g1, w_dt, dt_b, a_neg)

        u = _conv_silu(proj3, conv_w[i], conv_b[i][None, :], 256)
        y_ssd = _ssd_scan(u, ac, act, dtt, jnp.repeat(d_skip[i], SSM_HEAD_DIM)[None, :])

        gqk = jnp.tile(qk_norm_g[i], (1, 2))
        attn = _diff_attention(proj3, bias_win, rel_bias.T.astype(F32), gqk, lambda_qk[i],
                               attn_head_norm_g[i][None, :], tq, lam_init)

        w_r = jnp.pad(jnp.concatenate([w_router_group[i], w_router_expert[i]], axis=1),
                      ((0, 0), (0, LANES - MOE_GROUPS - MOE_EXPERTS)))
        b_r = jnp.pad(jnp.concatenate([b_router_group[i], b_router_expert[i]]),
                      (0, LANES - MOE_GROUPS - MOE_EXPERTS))[None, :]
        x_mid, hn, ids, gates = _post_mixer(
            attn.reshape(t, ATTN_WIDTH), y_ssd.reshape(t, SSM_INNER), proj, xf, ssm_norm_g[i][None, :],
            w_attn_out[i].astype(BF16), w_ssm_out[i].astype(BF16), w_out[i].astype(BF16), norm2_g[i][None, :],
            w_r, b_r, _pick(t, 256))

        blk_expert, nused, pos_kt, seg = _dispatch_plan(ids, _pick(t, 256))
        y_pad = _moe_experts(blk_expert, pos_kt, nused, seg, hn, w_exp_gate, w_exp_up, w_exp_down, i)
        xf = _moe_combine(pos_kt, x_mid, gates, y_pad, _pick(t, 128))
    return xf.reshape(b, s, d)
```

```python
import functools
import math

import jax
import jax.numpy as jnp
import numpy as np
from jax import lax
from jax.experimental import pallas as pl
from jax.experimental.pallas import tpu as pltpu

D_MODEL = 1024
DEPTH = 2
ATTN_HEADS = 8
ATTN_HEAD_DIM = 64
ATTN_V_DIM = 2 * ATTN_HEAD_DIM
ATTN_WIDTH = ATTN_HEADS * ATTN_V_DIM
NUM_BUCKETS = 32
MAX_DISTANCE = 128
SSM_INNER = 2 * D_MODEL
SSM_HEAD_DIM = 64
SSM_HEADS = SSM_INNER // SSM_HEAD_DIM
SSM_GROUPS = 4
SSM_HEADS_PER_GROUP = SSM_HEADS // SSM_GROUPS
SSM_STATE = 128
SSM_CONV = 5
SSM_CHUNK = 128
SSM_CONV_CH = SSM_INNER + 2 * SSM_GROUPS * SSM_STATE
MOE_GROUPS = 4
MOE_EXPERTS_PER_GROUP = 8
MOE_EXPERTS = MOE_GROUPS * MOE_EXPERTS_PER_GROUP
MOE_TOPK = 2
MOE_FF = 512
EXPERT_ROWS = 128
RMS_EPS = 1e-6

LANES = 128
COL_Q = 0
COL_K = COL_Q + ATTN_WIDTH
COL_V = COL_K + ATTN_WIDTH
COL_Z = COL_V + ATTN_WIDTH
COL_XBC = COL_Z + SSM_INNER
MAIN_COLS = COL_XBC + SSM_CONV_CH
DT_COLS = 2 * SSM_HEADS
COL_GATES = MAIN_COLS + DT_COLS
COL_GA = MAIN_COLS
COL_GM = COL_GA + D_MODEL
PROJ_COLS = COL_GM + D_MODEL
DT_LANES_PER_GROUP = 2 * SSM_HEADS_PER_GROUP
PACKED = D_MODEL // 2

VMEM_LIMIT_BYTES = 56 * 1024 * 1024
LOG2E = 1.4426950408889634

F32 = jnp.float32
BF16 = jnp.bfloat16


def _params(*semantics):
    return pltpu.CompilerParams(dimension_semantics=semantics, vmem_limit_bytes=VMEM_LIMIT_BYTES)


def _sigmoid(x):
    return 0.5 * jnp.tanh(0.5 * x) + 0.5


def _silu(x):
    h = 0.5 * x
    return h + h * jnp.tanh(h)


def _rms(x, g):
    ms = jnp.mean(x * x, axis=-1, keepdims=True)
    return x * lax.rsqrt(ms + RMS_EPS) * g


def _norm_matmul_kernel(x_ref, g_ref, w_ref, wg_ref, o_ref, hn_ref, *, n_main):
    j = pl.program_id(1)

    @pl.when(j == 0)
    def _():
        hn_ref[...] = _rms(x_ref[...], g_ref[...]).astype(hn_ref.dtype)

    @pl.when(j < n_main)
    def _():
        o_ref[...] = jnp.dot(hn_ref[...], w_ref[...].astype(BF16), preferred_element_type=F32).astype(o_ref.dtype)

    @pl.when(j >= n_main)
    def _():
        o_ref[...] = jnp.dot(hn_ref[...], wg_ref[...].astype(BF16), preferred_element_type=F32).astype(o_ref.dtype)


def _norm_matmul(x, g, w, layer, w_gates, out_dtype, tm, tn):
    t, d = x.shape
    n_main = MAIN_COLS // tn
    n_gate = w_gates.shape[1] // tn
    return pl.pallas_call(
        functools.partial(_norm_matmul_kernel, n_main=n_main),
        grid=(t // tm, n_main + n_gate),
        in_specs=[
            pl.BlockSpec((tm, d), lambda i, j: (i, 0)),
            pl.BlockSpec((1, d), lambda i, j: (0, 0)),
            pl.BlockSpec((None, d, tn), lambda i, j: (layer, 0, jnp.minimum(j, n_main - 1))),
            pl.BlockSpec((d, tn), lambda i, j: (0, jnp.maximum(j - n_main, 0))),
        ],
        out_specs=pl.BlockSpec((tm, tn), lambda i, j: (i, j)),
        out_shape=jax.ShapeDtypeStruct((t, (n_main + n_gate) * tn), out_dtype),
        scratch_shapes=[pltpu.VMEM((tm, d), BF16)],
        compiler_params=_params("parallel", "arbitrary"),
        name="in_proj",
    )(x, g, w, w_gates)


def _dt_prep_kernel(x_ref, g_ref, w_ref, bias_ref, a_ref, ac_ref, act_ref, dtt_ref):
    hn = _rms(x_ref[...], g_ref[...]).astype(BF16)
    raw = jnp.dot(hn, w_ref[...], preferred_element_type=F32) + bias_ref[...]
    dt = jnp.maximum(raw, 0.0) + jnp.log1p(jnp.exp(-jnp.abs(raw)))
    a = dt * (a_ref[...] * LOG2E)
    s = a.shape[0]
    pos = lax.broadcasted_iota(jnp.int32, a.shape, 0) & (SSM_CHUNK - 1)
    pre = a
    suf = a
    k = 1
    while k < SSM_CHUNK:
        pre = pre + jnp.where(pos >= k, pltpu.roll(pre, k, 0), 0.0)
        suf = suf + jnp.where(pos < SSM_CHUNK - k, pltpu.roll(suf, s - k, 0), 0.0)
        k *= 2
    lane = lax.broadcasted_iota(jnp.int32, a.shape, 1)
    is_fwd = (lane & (DT_LANES_PER_GROUP - 1)) < SSM_HEADS_PER_GROUP
    ac = jnp.where(is_fwd, pre, suf)
    act_ref[...] = ac.T
    dtt_ref[...] = dt.T
    for grp in range(SSM_GROUPS):
        shift = (LANES - grp * DT_LANES_PER_GROUP) % LANES
        ac_ref[:, grp * LANES:(grp + 1) * LANES] = ac if shift == 0 else pltpu.roll(ac, shift, 1)


def _dt_prep(x3, g, w_dt, bias, a_neg):
    b, s, d = x3.shape
    out = jax.ShapeDtypeStruct((b, s, SSM_GROUPS * LANES), F32)
    out_t = jax.ShapeDtypeStruct((b, LANES, s), F32)
    vec = pl.BlockSpec((1, LANES), lambda i: (0, 0))
    return pl.pallas_call(
        _dt_prep_kernel,
        grid=(b,),
        in_specs=[
            pl.BlockSpec((None, s, d), lambda i: (i, 0, 0)),
            pl.BlockSpec((1, d), lambda i: (0, 0)),
            pl.BlockSpec((d, LANES), lambda i: (0, 0)),
            vec,
            vec,
        ],
        out_specs=[
            pl.BlockSpec((None, s, SSM_GROUPS * LANES), lambda i: (i, 0, 0)),
            pl.BlockSpec((None, LANES, s), lambda i: (i, 0, 0)),
            pl.BlockSpec((None, LANES, s), lambda i: (i, 0, 0)),
        ],
        out_shape=[out, out_t, out_t],
        compiler_params=_params("parallel"),
        name="dt_prep",
    )(x3, g, w_dt, bias, a_neg)


def _conv_kernel(x_ref, w_ref, b_ref, o_ref):
    xf = x_ref[...].astype(F32)
    s, tc = xf.shape
    pad = SSM_CONV // 2
    xe = jnp.concatenate([xf, jnp.zeros((8, tc), F32)], axis=0)
    acc = xf * w_ref[pad:pad + 1, :] + b_ref[...]
    for j in range(SSM_CONV):
        shift = pad - j
        if shift != 0:
            acc = acc + pltpu.roll(xe, shift % (s + 8), 0)[:s, :] * w_ref[j:j + 1, :]
    o_ref[...] = _silu(acc).astype(o_ref.dtype)


def _conv_silu(proj3, conv_w, conv_b, tc):
    b, s, _ = proj3.shape
    col0 = COL_XBC // tc
    return pl.pallas_call(
        _conv_kernel,
        grid=(b, SSM_CONV_CH // tc),
        in_specs=[
            pl.BlockSpec((None, s, tc), lambda i, c: (i, 0, col0 + c)),
            pl.BlockSpec((SSM_CONV, tc), lambda i, c: (0, c)),
            pl.BlockSpec((1, tc), lambda i, c: (0, c)),
        ],
        out_specs=pl.BlockSpec((None, s, tc), lambda i, c: (i, 0, c)),
        out_shape=jax.ShapeDtypeStruct((b, s, SSM_CONV_CH), BF16),
        compiler_params=_params("parallel", "parallel"),
        name="conv_silu",
    )(proj3, conv_w, conv_b)


def _rel_bucket(rel):
    half = NUM_BUCKETS // 2
    max_exact = half // 2
    ret = jnp.where(rel > 0, half, 0)
    n = jnp.abs(rel)
    n_safe = jnp.maximum(n, 1).astype(F32)
    large = max_exact + (jnp.log(n_safe / max_exact) / math.log(MAX_DISTANCE / max_exact)
                         * (half - max_exact)).astype(jnp.int32)
    large = jnp.minimum(large, half - 1)
    return ret + jnp.where(n < max_exact, n, large)


def _bias_kernel(relb_ref, bid_ref, o_ref, *, tq):
    h = pl.program_id(0)
    bid = bid_ref[...]
    fv = jnp.zeros(bid.shape, F32)
    for bkt in range(NUM_BUCKETS):
        fv = jnp.where(bid == bkt, relb_ref[h, bkt], fv)
    base = min(tq, 2 * LANES)
    g = jnp.broadcast_to(fv, (base, bid.shape[1]))
    row = lax.broadcasted_iota(jnp.int32, g.shape, 0)
    k = 1
    while k < base:
        g = jnp.where((row & k) != 0, pltpu.roll(g, k, 1), g)
        k *= 2
    for r0 in range(0, tq, base):
        o_ref[r0:r0 + base, :] = g if r0 == 0 else pltpu.roll(g, r0, 1)


def _bias_windows(rel_bias, s, tq):
    width = 2 * s
    m = jnp.arange(width, dtype=jnp.int32)
    bid = _rel_bucket(((m + tq) % width) - s).astype(jnp.int32)[None, :]
    return pl.pallas_call(
        functools.partial(_bias_kernel, tq=tq),
        grid=(ATTN_HEADS,),
        in_specs=[
            pl.BlockSpec(memory_space=pltpu.SMEM),
            pl.BlockSpec((1, width), lambda h: (0, 0)),
        ],
        out_specs=pl.BlockSpec((None, tq, width), lambda h: (h, 0, 0)),
        out_shape=jax.ShapeDtypeStruct((ATTN_HEADS, tq, width), F32),
        compiler_params=_params("parallel"),
        name="rel_bias_windows",
    )(rel_bias.T.astype(F32), bid)


SOFTMAX_SHIFT_SLACK = 40.0
BF16_ROUNDING_SLACK = 1.01


def _attn_kernel(relb_ref, q_ref, k_ref, v_ref, bias_ref, gqk_ref, lamqk_ref, gh_ref, o_ref, q0_ref, q1_ref, kn_ref,
                 vx_ref, scal_ref, *, tq, lam_init):
    h = pl.program_id(0)
    qi = pl.program_id(2)
    s = k_ref.shape[0]
    lo = lax.broadcasted_iota(jnp.int32, (1, LANES), 1) < ATTN_HEAD_DIM

    def half_sums(x2):
        s_all = jnp.sum(x2, axis=-1, keepdims=True)
        s_lo = jnp.sum(jnp.where(lo, x2, 0.0), axis=-1, keepdims=True)
        return s_lo, jnp.maximum(s_all - s_lo, 0.0)

    def half_norm(x, g):
        s_lo, s_hi = half_sums(x * x)
        ms = jnp.where(lo, s_lo, s_hi) * (1.0 / ATTN_HEAD_DIM)
        return x * lax.rsqrt(ms + RMS_EPS) * g

    @pl.when(qi == 0)
    def _():
        kn_ref[...] = half_norm(k_ref[...].astype(F32), gqk_ref[1:2, :]).astype(BF16)
        vx_ref[:, :LANES] = v_ref[...]
        vx_ref[:, LANES:] = jnp.ones((s, LANES), BF16)
        bmax = relb_ref[h, 0]
        bmin = relb_ref[h, 0]
        for bkt in range(1, NUM_BUCKETS):
            bmax = jnp.maximum(bmax, relb_ref[h, bkt])
            bmin = jnp.minimum(bmin, relb_ref[h, bkt])
        qn = (half_norm(q_ref[...].astype(F32), gqk_ref[0:1, :]) * (ATTN_HEAD_DIM ** -0.5)).astype(BF16)
        q0_ref[...] = jnp.where(lo, qn, jnp.zeros_like(qn))
        q1_ref[...] = jnp.where(lo, jnp.zeros_like(qn), qn)
        gains = jnp.abs(gqk_ref[...])
        reach = (BF16_ROUNDING_SLACK * ATTN_HEAD_DIM ** 0.5) * jnp.max(gains[0:1, :]) * jnp.max(gains[1:2, :])
        scal_ref[0] = reach + bmax
        scal_ref[1] = 2.0 * reach + (bmax - bmin)

    rows = pl.ds(pl.multiple_of(qi * tq, tq), tq)
    q0 = q0_ref[rows, :]
    q1 = q1_ref[rows, :]
    shift = scal_ref[0]
    slack = scal_ref[1]
    c0 = pl.multiple_of(s - (qi + 1) * tq, tq)
    lq = lamqk_ref[...]
    lam = (jnp.exp(jnp.sum(lq[0:1] * lq[1:2], axis=-1, keepdims=True))
           - jnp.exp(jnp.sum(lq[2:3] * lq[3:4], axis=-1, keepdims=True)) + lam_init)

    def logits_of(qm):
        return lax.dot_general(qm, kn_ref[...], (((1,), (1,)), ((), ())),
                               preferred_element_type=F32) + bias_ref[:, pl.ds(c0, s)]

    def finish(out):
        o_ref[...] = (_rms(out, gh_ref[...]) * (1.0 - lam_init)).astype(o_ref.dtype)

    @pl.when(slack <= SOFTMAX_SHIFT_SLACK)
    def _():
        def weighted_v(qm):
            p = jnp.exp(logits_of(qm) - shift).astype(BF16)
            return jnp.dot(p, vx_ref[...], preferred_element_type=F32)

        o0 = weighted_v(q0)
        o1 = weighted_v(q1)
        finish(o0[:, :LANES] / o0[:, LANES:LANES + 1] - o1[:, :LANES] * (lam / o1[:, LANES:LANES + 1]))

    @pl.when(slack > SOFTMAX_SHIFT_SLACK)
    def _():
        def softmax_map(qm):
            logits = logits_of(qm)
            p = jnp.exp(logits - jnp.max(logits, axis=-1, keepdims=True))
            return p, jnp.sum(p, axis=-1, keepdims=True)

        p0, l0 = softmax_map(q0)
        p1, l1 = softmax_map(q1)
        attn = p0 * (1.0 / l0) - p1 * (lam / l1)
        finish(jnp.dot(attn.astype(BF16), v_ref[...], preferred_element_type=F32))


def _diff_attention(proj3, bias_win, relb_t, gqk, lamqk, gh, tq, lam_init):
    b, s, _ = proj3.shape
    width = bias_win.shape[-1]
    return pl.pallas_call(
        functools.partial(_attn_kernel, tq=tq, lam_init=lam_init),
        grid=(ATTN_HEADS, b, s // tq),
        in_specs=[
            pl.BlockSpec(memory_space=pltpu.SMEM),
            pl.BlockSpec((None, s, LANES), lambda h, i, q: (i, 0, COL_Q // LANES + h)),
            pl.BlockSpec((None, s, LANES), lambda h, i, q: (i, 0, COL_K // LANES + h)),
            pl.BlockSpec((None, s, LANES), lambda h, i, q: (i, 0, COL_V // LANES + h)),
            pl.BlockSpec((None, tq, width), lambda h, i, q: (h, 0, 0)),
            pl.BlockSpec((2, LANES), lambda h, i, q: (0, 0)),
            pl.BlockSpec((4, ATTN_HEAD_DIM), lambda h, i, q: (0, 0)),
            pl.BlockSpec((1, LANES), lambda h, i, q: (0, 0)),
        ],
        out_specs=pl.BlockSpec((None, tq, LANES), lambda h, i, q: (i, q, h)),
        out_shape=jax.ShapeDtypeStruct((b, s, ATTN_WIDTH), BF16),
        scratch_shapes=[pltpu.VMEM((s, LANES), BF16), pltpu.VMEM((s, LANES), BF16), pltpu.VMEM((s, LANES), BF16),
                        pltpu.VMEM((s, 2 * LANES), BF16), pltpu.SMEM((2,), F32)],
        compiler_params=_params("parallel", "parallel", "arbitrary"),
        name="diff_attention",
    )(relb_t, proj3, proj3, proj3, bias_win, gqk, lamqk, gh)


def _ssd_kernel(x_ref, b_ref, c_ref, ac_ref, act_ref, dtt_ref, dsk_ref, y_ref, st_ref, acc_ref, bt_ref):
    L = SSM_CHUNK
    s = x_ref.shape[0]
    n_chunks = s // L
    row = lax.broadcasted_iota(jnp.int32, (L, L), 0)
    col = lax.broadcasted_iota(jnp.int32, (L, L), 1)
    lo = lax.broadcasted_iota(jnp.int32, (L, LANES), 1) < SSM_HEAD_DIM
    lo1 = lo[0:1, :]
    n_pairs = SSM_HEADS_PER_GROUP // 2

    bt_ref[...] = b_ref[...].astype(F32).T.astype(BF16)

    def one_chunk(c, backward):
        d = 1 if backward else 0
        rows = pl.ds(pl.multiple_of(c * L, L), L)
        off = SSM_HEADS_PER_GROUP * d
        tot_row = 0 if backward else L - 1
        mask = (row <= col) if backward else (row >= col)
        xb = x_ref[rows, :]
        cc = c_ref[rows, :]
        acc = ac_ref[rows, :]
        act = act_ref[:, rows]
        dtt = dtt_ref[:, rows]
        bt = bt_ref[:, rows].astype(F32)
        cb = lax.dot_general(cc, b_ref[rows, :], (((1,), (1,)), ((), ())), preferred_element_type=F32)

        ys = []
        for j in range(n_pairs):
            lanes = slice(j * LANES, (j + 1) * LANES)
            x_pair = xb[:, lanes]
            y_off = jnp.dot(cc, st_ref[d, :, lanes].astype(BF16), preferred_element_type=F32)
            diag, stn, acol, tot = [], [], [], []
            for e in (2 * j, 2 * j + 1):
                a_col = jnp.broadcast_to(acc[:, off + e:off + e + 1], (L, LANES))
                a_row = act[off + e:off + e + 1, :]
                dt_row = dtt[off + e:off + e + 1, :]
                tot_e = acc[tot_row:tot_row + 1, off + e:off + e + 1]
                decay = jnp.where(mask, jnp.exp2(a_col - a_row), 0.0)
                diag.append(jnp.dot((cb * decay * dt_row).astype(BF16), x_pair, preferred_element_type=F32))
                w_row = dt_row * jnp.exp2(tot_e - a_row)
                stn.append(jnp.dot((bt * w_row).astype(BF16), x_pair, preferred_element_type=F32))
                acol.append(a_col)
                tot.append(tot_e)
            ys.append(jnp.where(lo, diag[0], diag[1]) + y_off * jnp.exp2(jnp.where(lo, acol[0], acol[1])))
            st_ref[d, :, lanes] = (st_ref[d, :, lanes] * jnp.exp2(jnp.where(lo1, tot[0], tot[1]))
                                   + jnp.where(lo, stn[0], stn[1]))
        y = jnp.concatenate(ys, axis=1)
        if not backward:
            y = y + xb.astype(F32) * dsk_ref[...]
        acc_ref[d, rows, :] = y

    st_ref[...] = jnp.zeros_like(st_ref)

    def body(c, carry):
        one_chunk(c, False)
        one_chunk(n_chunks - 1 - c, True)
        return carry

    lax.fori_loop(0, n_chunks, body, 0)
    y_ref[...] = (acc_ref[0] + acc_ref[1]).astype(y_ref.dtype)


def _ssd_scan(u, ac_g, act_g, dtt_g, dskip):
    b, s, _ = u.shape
    gw = SSM_HEADS_PER_GROUP * SSM_HEAD_DIM
    xcols = SSM_INNER // SSM_STATE
    nl = DT_LANES_PER_GROUP
    return pl.pallas_call(
        _ssd_kernel,
        grid=(b, SSM_GROUPS),
        in_specs=[
            pl.BlockSpec((None, s, gw), lambda i, g: (i, 0, g)),
            pl.BlockSpec((None, s, SSM_STATE), lambda i, g: (i, 0, xcols + g)),
            pl.BlockSpec((None, s, SSM_STATE), lambda i, g: (i, 0, xcols + SSM_GROUPS + g)),
            pl.BlockSpec((None, s, LANES), lambda i, g: (i, 0, g)),
            pl.BlockSpec((None, nl, s), lambda i, g: (i, g, 0)),
            pl.BlockSpec((None, nl, s), lambda i, g: (i, g, 0)),
            pl.BlockSpec((1, gw), lambda i, g: (0, g)),
        ],
        out_specs=pl.BlockSpec((None, s, gw), lambda i, g: (i, 0, g)),
        out_shape=jax.ShapeDtypeStruct((b, s, SSM_INNER), BF16),
        scratch_shapes=[pltpu.VMEM((2, SSM_STATE, gw), F32), pltpu.VMEM((2, s, gw), F32),
                        pltpu.VMEM((SSM_STATE, s), BF16)],
        compiler_params=_params("parallel", "parallel"),
        name="ssd_scan",
    )(u, u, u, ac_g, act_g, dtt_g, dskip)


def _route(logits):
    lane = lax.broadcasted_iota(jnp.int32, logits.shape, 1)
    lane_f = lane.astype(F32)
    neg = -jnp.inf
    big = float(LANES)
    is_g = lane < MOE_GROUPS
    gl = jnp.where(is_g, logits, neg)
    gmax = jnp.max(gl, axis=-1, keepdims=True)
    g_idx = jnp.min(jnp.where(gl == gmax, lane_f, big), axis=-1, keepdims=True)
    g_gate = 1.0 / jnp.sum(jnp.where(is_g, jnp.exp(logits - gmax), 0.0), axis=-1, keepdims=True)
    first = MOE_GROUPS + MOE_EXPERTS_PER_GROUP * g_idx
    in_grp = (lane_f >= first) & (lane_f < first + MOE_EXPERTS_PER_GROUP)
    el = jnp.where(in_grp, logits, neg)
    v1 = jnp.max(el, axis=-1, keepdims=True)
    i1 = jnp.min(jnp.where(el == v1, lane_f, big), axis=-1, keepdims=True)
    el2 = jnp.where(lane_f == i1, neg, el)
    v2 = jnp.max(el2, axis=-1, keepdims=True)
    i2 = jnp.min(jnp.where(el2 == v2, lane_f, big), axis=-1, keepdims=True)
    r = jnp.exp(v2 - v1)
    p1 = 1.0 / (1.0 + r)
    p2 = r * p1
    ids = jnp.where(lane == 0, i1 - MOE_GROUPS, jnp.where(lane == 1, i2 - MOE_GROUPS, 0.0)).astype(jnp.int32)
    gates = jnp.where(lane == 0, p1 * g_gate, jnp.where(lane == 1, p2 * g_gate, 0.0))
    return ids, gates


_HI_HALF = 0xFFFF0000


def _pack_bf16_pairs(x):
    n = x.shape[1] // 2
    bits = lax.bitcast_convert_type(x.astype(BF16).astype(F32), jnp.uint32)
    return (bits[:, n:] & jnp.uint32(_HI_HALF)) | (bits[:, :n] >> 16)


def _unpack_bf16_pairs(w):
    lo = lax.bitcast_convert_type(w << 16, F32)
    hi = lax.bitcast_convert_type(w & jnp.uint32(_HI_HALF), F32)
    return jnp.concatenate([lo, hi], axis=1)


def _rank_assignments(ids, carry):
    tt = ids.shape[0]
    lane = lax.broadcasted_iota(jnp.int32, ids.shape, 1)
    oh0 = jnp.where(lane == ids[:, 0:1], 1.0, 0.0)
    oh1 = jnp.where(lane == ids[:, 1:2], 1.0, 0.0)
    earlier = jnp.where(lax.broadcasted_iota(jnp.int32, (tt, tt), 1) < lax.broadcasted_iota(jnp.int32, (tt, tt), 0),
                        1.0, 0.0).astype(BF16)
    before0 = jnp.dot(earlier, oh0.astype(BF16), preferred_element_type=F32)
    before1 = jnp.dot(earlier, oh1.astype(BF16), preferred_element_type=F32)
    tot0 = jnp.sum(oh0, axis=0, keepdims=True)
    rank0 = jnp.sum(oh0 * (before0 + carry), axis=-1, keepdims=True)
    rank1 = jnp.sum(oh1 * (before1 + tot0 + carry), axis=-1, keepdims=True)
    rank = jnp.where(lane == 0, rank0, jnp.where(lane == 1, rank1, 0.0)).astype(jnp.int32)
    return rank, carry + tot0 + jnp.sum(oh1, axis=0, keepdims=True)


def _post_kernel(attn_ref, y_ref, z0_ref, z1_ref, ga_ref, gm_ref, x_ref, gs_ref, wa_ref, ws_ref, wo_ref, g2_ref,
                 wr_ref, br_ref, xo_ref, hn_ref, id_ref, gate_ref, rank_ref, cnt_ref, carry_ref):
    @pl.when(pl.program_id(0) == 0)
    def _():
        carry_ref[...] = jnp.zeros_like(carry_ref)

    ya = jnp.dot(attn_ref[...], wa_ref[...], preferred_element_type=F32)
    z = jnp.concatenate([z0_ref[...], z1_ref[...]], axis=1).astype(F32)
    gated = y_ref[...].astype(F32) * _silu(z)
    ym = jnp.dot(_rms(gated, gs_ref[...]).astype(BF16), ws_ref[...], preferred_element_type=F32)
    merged = (_sigmoid(ga_ref[...].astype(F32)) * ya + _sigmoid(gm_ref[...].astype(F32)) * ym)
    x_new = x_ref[...] + jnp.dot(merged.astype(BF16), wo_ref[...], preferred_element_type=F32)
    xo_ref[...] = x_new
    hn = _rms(x_new, g2_ref[...])
    hn_ref[:, 0, :] = _pack_bf16_pairs(hn)
    hn_hi = hn.astype(BF16)
    hn_lo = (hn - hn_hi.astype(F32)).astype(BF16)
    wr = wr_ref[...]
    wr_hi = wr.astype(BF16)
    wr_lo = (wr - wr_hi.astype(F32)).astype(BF16)
    logits = (jnp.dot(hn_hi, wr_hi, preferred_element_type=F32) + jnp.dot(hn_lo, wr_hi, preferred_element_type=F32)
              + jnp.dot(hn_hi, wr_lo, preferred_element_type=F32) + br_ref[...])
    ids, gates = _route(logits)
    id_ref[...] = ids
    gate_ref[...] = gates
    rank, carry = _rank_assignments(ids, carry_ref[...])
    rank_ref[...] = rank
    carry_ref[...] = carry
    cnt_ref[...] = carry


def _post_mixer(attn, y, proj, x, gs, wa, ws, wo, g2, wr, br, tm):
    t, d = x.shape
    row = lambda w, c: pl.BlockSpec((tm, w), lambda i: (i, c))
    full = lambda a: pl.BlockSpec(a.shape, lambda i: (0,) * a.ndim)
    return pl.pallas_call(
        _post_kernel,
        grid=(t // tm,),
        in_specs=[
            row(ATTN_WIDTH, 0),
            row(SSM_INNER, 0),
            row(D_MODEL, COL_Z // D_MODEL),
            row(D_MODEL, COL_Z // D_MODEL + 1),
            row(D_MODEL, COL_GA // D_MODEL),
            row(D_MODEL, COL_GM // D_MODEL),
            row(D_MODEL, 0),
            full(gs), full(wa), full(ws), full(wo), full(g2), full(wr), full(br),
        ],
        out_specs=[row(D_MODEL, 0), pl.BlockSpec((tm, 1, PACKED), lambda i: (i, 0, 0)), row(LANES, 0), row(LANES, 0),
                   row(LANES, 0), pl.BlockSpec((1, LANES), lambda i: (0, 0))],
        out_shape=[
            jax.ShapeDtypeStruct((t, d), F32),
            jax.ShapeDtypeStruct((t, 1, PACKED), jnp.uint32),
            jax.ShapeDtypeStruct((t, LANES), jnp.int32),
            jax.ShapeDtypeStruct((t, LANES), F32),
            jax.ShapeDtypeStruct((t, LANES), jnp.int32),
            jax.ShapeDtypeStruct((1, LANES), F32),
        ],
        scratch_shapes=[pltpu.VMEM((1, LANES), F32)],
        compiler_params=_params("arbitrary"),
        name="post_mixer",
    )(attn, y, proj, proj, proj, proj, x, gs, wa, ws, wo, g2, wr, br)


GATHER_DEPTH = 3


def _start_rows(idx_ref, base, src_hbm, dst_ref, sem, rows=None):
    for r in (range(dst_ref.shape[0]) if rows is None else rows):
        pltpu.make_async_copy(src_hbm.at[idx_ref[base + r]], dst_ref.at[pl.ds(r, 1), :], sem).start(priority=r % 2)


def _wait_buffer(dst_ref, sem):
    pltpu.make_async_copy(dst_ref, dst_ref, sem).wait()


def _moe_kernel(blk_e_ref, pos_ref, nused_ref, seg_ref, hn_hbm, wg_ref, wu_ref, wd_ref, y_ref, src_ref, xg_ref,
                wgb_ref, wub_ref, wdb_ref, sem, *, n_tokens):
    i = pl.program_id(0)
    nused = nused_ref[0]
    quarter = EXPERT_ROWS // 4

    def start(blk, part=None):
        slot = blk % GATHER_DEPTH
        rows = range(EXPERT_ROWS) if part is None else range(part * quarter, (part + 1) * quarter)
        _start_rows(src_ref, blk * EXPERT_ROWS, hn_hbm, xg_ref.at[slot], sem.at[slot], rows)

    @pl.when(i == 0)
    def _():
        def clear(j, carry):
            src_ref[j] = 0
            return carry

        for e in range(MOE_EXPERTS):
            lax.fori_loop(seg_ref[e], seg_ref[MOE_EXPERTS + e], clear, 0)
        for k in range(MOE_TOPK):
            def fill(tok, carry):
                src_ref[pos_ref[k * n_tokens + tok]] = tok
                return carry

            lax.fori_loop(0, n_tokens, fill, 0, unroll=8)
        start(0)

    @pl.when((i == 0) & (nused > 1))
    def _():
        start(1)

    new_expert = (i == 0) | (blk_e_ref[i] != blk_e_ref[jnp.maximum(i - 1, 0)])

    @pl.when((i < nused) & new_expert)
    def _():
        wgb_ref[...] = wg_ref[...].astype(BF16)
        wub_ref[...] = wu_ref[...].astype(BF16)
        wdb_ref[...] = wd_ref[...].astype(BF16)

    def compute(prefetch):
        xb = _unpack_bf16_pairs(xg_ref[i % GATHER_DEPTH]).astype(BF16)
        prefetch(0)
        hg = jnp.dot(xb, wgb_ref[...], preferred_element_type=F32)
        prefetch(1)
        hu = jnp.dot(xb, wub_ref[...], preferred_element_type=F32)
        prefetch(2)
        hid = (_silu(hg) * hu).astype(BF16)
        y = jnp.dot(hid, wdb_ref[...], preferred_element_type=F32)
        prefetch(3)
        y_ref[:, 0, :] = _pack_bf16_pairs(y)

    def wait():
        slot = i % GATHER_DEPTH
        _wait_buffer(xg_ref.at[slot], sem.at[slot])

    @pl.when(i + 2 < nused)
    def _():
        wait()
        compute(lambda part: start(i + 2, part))

    @pl.when((i < nused) & (i + 2 >= nused))
    def _():
        wait()
        compute(lambda part: None)

    @pl.when(i >= nused)
    def _():
        y_ref[...] = jnp.zeros_like(y_ref)


def _moe_experts(blk_expert, pos_kt, nused, seg, hn3, wg, wu, wd, layer):
    n_blocks = blk_expert.shape[0]
    d = wg.shape[2]
    n_tokens = hn3.shape[0]
    grid_spec = pltpu.PrefetchScalarGridSpec(
        num_scalar_prefetch=4,
        grid=(n_blocks,),
        in_specs=[
            pl.BlockSpec(memory_space=pl.ANY),
            pl.BlockSpec((None, None, d, MOE_FF), lambda i, be, st, nu, sg: (layer, be[i], 0, 0)),
            pl.BlockSpec((None, None, d, MOE_FF), lambda i, be, st, nu, sg: (layer, be[i], 0, 0)),
            pl.BlockSpec((None, None, MOE_FF, d), lambda i, be, st, nu, sg: (layer, be[i], 0, 0)),
        ],
        out_specs=pl.BlockSpec((EXPERT_ROWS, 1, PACKED), lambda i, be, st, nu, sg: (i, 0, 0)),
        scratch_shapes=[
            pltpu.SMEM((n_blocks * EXPERT_ROWS,), jnp.int32),
            pltpu.VMEM((GATHER_DEPTH, EXPERT_ROWS, PACKED), jnp.uint32),
            pltpu.VMEM((d, MOE_FF), BF16),
            pltpu.VMEM((d, MOE_FF), BF16),
            pltpu.VMEM((MOE_FF, d), BF16),
            pltpu.SemaphoreType.DMA((GATHER_DEPTH,)),
        ],
    )
    return pl.pallas_call(
        functools.partial(_moe_kernel, n_tokens=n_tokens),
        grid_spec=grid_spec,
        out_shape=jax.ShapeDtypeStruct((n_blocks * EXPERT_ROWS, 1, PACKED), jnp.uint32),
        compiler_params=_params("arbitrary"),
        name="moe_experts",
    )(blk_expert, pos_kt, nused, seg, hn3, wg, wu, wd)


def _combine_kernel(pos_ref, x_ref, gate_ref, y_hbm, o_ref, yg_ref, sem, *, tt, n_tiles):
    i = pl.program_id(0)

    def start(tile):
        slot = tile % GATHER_DEPTH
        for k in range(MOE_TOPK):
            _start_rows(pos_ref, (k * n_tiles + tile) * tt, y_hbm, yg_ref.at[slot, k], sem.at[slot])

    def finish():
        gates = gate_ref[...]
        slot = i % GATHER_DEPTH
        o_ref[...] = (x_ref[...] + gates[:, 0:1] * _unpack_bf16_pairs(yg_ref[slot, 0])
                      + gates[:, 1:2] * _unpack_bf16_pairs(yg_ref[slot, 1]))

    def wait():
        slot = i % GATHER_DEPTH
        _wait_buffer(yg_ref.at[slot], sem.at[slot])

    @pl.when(i == 0)
    def _():
        start(0)
        if n_tiles > 1:
            start(1)

    @pl.when(i + 2 < n_tiles)
    def _():
        wait()
        start(i + 2)
        finish()

    @pl.when(i + 2 >= n_tiles)
    def _():
        wait()
        finish()


def _moe_combine(pos_kt, x, gates, y3, tt):
    t, d = x.shape
    n_tiles = t // tt
    grid_spec = pltpu.PrefetchScalarGridSpec(
        num_scalar_prefetch=1,
        grid=(n_tiles,),
        in_specs=[
            pl.BlockSpec((tt, d), lambda i, p: (i, 0)),
            pl.BlockSpec((tt, LANES), lambda i, p: (i, 0)),
            pl.BlockSpec(memory_space=pl.ANY),
        ],
        out_specs=pl.BlockSpec((tt, d), lambda i, p: (i, 0)),
        scratch_shapes=[pltpu.VMEM((GATHER_DEPTH, MOE_TOPK, tt, PACKED), jnp.uint32),
                        pltpu.SemaphoreType.DMA((GATHER_DEPTH,))],
    )
    return pl.pallas_call(
        functools.partial(_combine_kernel, tt=tt, n_tiles=n_tiles),
        grid_spec=grid_spec,
        out_shape=jax.ShapeDtypeStruct((t, d), F32),
        compiler_params=_params("arbitrary"),
        name="moe_combine",
    )(pos_kt, x, gates, y3)


def _dispatch_plan(ids, rank, cnt):
    t = ids.shape[0]
    n_assign = t * MOE_TOPK
    counts = cnt[0, :MOE_EXPERTS].astype(jnp.int32)
    padded = (counts + EXPERT_ROWS - 1) // EXPERT_ROWS * EXPERT_ROWS
    pad_end = jnp.cumsum(padded)
    pad_start = pad_end - padded
    e_ids = ids[:, :MOE_TOPK]
    onehot = e_ids[:, :, None] == jnp.arange(MOE_EXPERTS, dtype=jnp.int32)[None, None, :]
    dest = jnp.sum(jnp.where(onehot, pad_start[None, None, :], 0), axis=-1) + rank[:, :MOE_TOPK]
    n_blocks = -(-n_assign // EXPERT_ROWS) + MOE_EXPERTS
    blk_row0 = jnp.arange(n_blocks, dtype=jnp.int32) * EXPERT_ROWS
    blk_expert = jnp.minimum(jnp.sum((pad_end[None, :] <= blk_row0[:, None]).astype(jnp.int32), axis=1),
                             MOE_EXPERTS - 1)
    nused = (pad_end[-1:] // EXPERT_ROWS).astype(jnp.int32)
    pos_kt = dest.T.reshape(-1).astype(jnp.int32)
    seg = jnp.concatenate([pad_start + counts, pad_end]).astype(jnp.int32)
    return blk_expert, nused, pos_kt, seg


def _pick(n, pref):
    return pref if n % pref == 0 else n


def kernel(x, norm1_g, w_in, qk_norm_g, lambda_qk, attn_head_norm_g, rel_bias, conv_w, conv_b, dt_bias, a_log,
           d_skip, ssm_norm_g, w_attn_out, w_ssm_out, w_out, norm2_g, w_router_group, b_router_group,
           w_router_expert, b_router_expert, w_exp_gate, w_exp_up, w_exp_down):
    b, s, d = x.shape
    t = b * s
    tq = _pick(s, 512)
    bias_win = _bias_windows(rel_bias, s, tq)

    lane = np.arange(DT_COLS)
    grp, rem = lane // DT_LANES_PER_GROUP, lane % DT_LANES_PER_GROUP
    direction, e = rem // SSM_HEADS_PER_GROUP, rem % SSM_HEADS_PER_GROUP
    dt_src = direction * SSM_HEADS + grp * SSM_HEADS_PER_GROUP + e
    pad_lanes = LANES - DT_COLS

    xf = x.reshape(t, d)
    for i in range(DEPTH):
        lam_init = 0.8 - 0.6 * math.exp(-0.3 * i)
        g1 = norm1_g[i][None, :]
        w_tail = lax.slice(w_in, (i, 0, MAIN_COLS), (i + 1, d, w_in.shape[2]))
        w_dt = jnp.pad(w_tail[0, :, :DT_COLS][:, dt_src], ((0, 0), (0, pad_lanes))).astype(BF16)
        dt_b = jnp.pad(dt_bias[i].reshape(-1)[dt_src], (0, pad_lanes))[None, :]
        a_neg = jnp.pad(-jnp.exp(a_log[i].astype(F32)).reshape(-1)[dt_src], (0, pad_lanes))[None, :]

        proj = _norm_matmul(xf, g1, w_in, i, w_tail[0, :, DT_COLS:], BF16, _pick(t, 2048), 1024)
        proj3 = proj.reshape(b, s, PROJ_COLS)
        ac, act, dtt = _dt_prep(xf.reshape(b, s, d), g1, w_dt, dt_b, a_neg)

        u = _conv_silu(proj3, conv_w[i], conv_b[i][None, :], 256)
        y_ssd = _ssd_scan(u, ac, act, dtt, jnp.repeat(d_skip[i], SSM_HEAD_DIM)[None, :])

        gqk = jnp.tile(qk_norm_g[i], (1, 2))
        attn = _diff_attention(proj3, bias_win, rel_bias.T.astype(F32), gqk, lambda_qk[i],
                               attn_head_norm_g[i][None, :], tq, lam_init)

        w_r = jnp.pad(jnp.concatenate([w_router_group[i], w_router_expert[i]], axis=1),
                      ((0, 0), (0, LANES - MOE_GROUPS - MOE_EXPERTS)))
        b_r = jnp.pad(jnp.concatenate([b_router_group[i], b_router_expert[i]]),
                      (0, LANES - MOE_GROUPS - MOE_EXPERTS))[None, :]
        x_mid, hn, ids, gates, rank, cnt = _post_mixer(
            attn.reshape(t, ATTN_WIDTH), y_ssd.reshape(t, SSM_INNER), proj, xf, ssm_norm_g[i][None, :],
            w_attn_out[i].astype(BF16), w_ssm_out[i].astype(BF16), w_out[i].astype(BF16), norm2_g[i][None, :],
            w_r, b_r, _pick(t, 256))

        blk_expert, nused, pos_kt, seg = _dispatch_plan(ids, rank, cnt)
        y_pad = _moe_experts(blk_expert, pos_kt, nused, seg, hn, w_exp_gate, w_exp_up, w_exp_down, i)
        xf = _moe_combine(pos_kt, x_mid, gates, y_pad, _pick(t, 512))
    return xf.reshape(b, s, d)
```

```python
import functools
import math

import jax
import jax.numpy as jnp
import numpy as np
from jax import lax
from jax.experimental import pallas as pl
from jax.experimental.pallas import tpu as pltpu

D_MODEL = 1024
DEPTH = 2
ATTN_HEADS = 8
ATTN_HEAD_DIM = 64
ATTN_V_DIM = 2 * ATTN_HEAD_DIM
ATTN_WIDTH = ATTN_HEADS * ATTN_V_DIM
NUM_BUCKETS = 32
MAX_DISTANCE = 128
SSM_INNER = 2 * D_MODEL
SSM_HEAD_DIM = 64
SSM_HEADS = SSM_INNER // SSM_HEAD_DIM
SSM_GROUPS = 4
SSM_HEADS_PER_GROUP = SSM_HEADS // SSM_GROUPS
SSM_STATE = 128
SSM_CONV = 5
SSM_CHUNK = 128
SSM_CONV_CH = SSM_INNER + 2 * SSM_GROUPS * SSM_STATE
MOE_GROUPS = 4
MOE_EXPERTS_PER_GROUP = 8
MOE_EXPERTS = MOE_GROUPS * MOE_EXPERTS_PER_GROUP
MOE_TOPK = 2
MOE_FF = 512
EXPERT_ROWS = 128
RMS_EPS = 1e-6

LANES = 128
COL_Q = 0
COL_K = COL_Q + ATTN_WIDTH
COL_V = COL_K + ATTN_WIDTH
COL_Z = COL_V + ATTN_WIDTH
COL_XBC = COL_Z + SSM_INNER
MAIN_COLS = COL_XBC + SSM_CONV_CH
DT_COLS = 2 * SSM_HEADS
COL_GATES = MAIN_COLS + DT_COLS
COL_GA = MAIN_COLS
COL_GM = COL_GA + D_MODEL
PROJ_COLS = COL_GM + D_MODEL
DT_LANES_PER_GROUP = 2 * SSM_HEADS_PER_GROUP
PACKED = D_MODEL // 2

VMEM_LIMIT_BYTES = 56 * 1024 * 1024
LOG2E = 1.4426950408889634

F32 = jnp.float32
BF16 = jnp.bfloat16


def _params(*semantics):
    return pltpu.CompilerParams(dimension_semantics=semantics, vmem_limit_bytes=VMEM_LIMIT_BYTES)


def _sigmoid(x):
    return 0.5 * jnp.tanh(0.5 * x) + 0.5


def _silu(x):
    h = 0.5 * x
    return h + h * jnp.tanh(h)


def _rms(x, g):
    ms = jnp.mean(x * x, axis=-1, keepdims=True)
    return x * lax.rsqrt(ms + RMS_EPS) * g


def _norm_matmul_kernel(x_ref, g_ref, w_ref, wg_ref, wdt_ref, o_ref, dt_ref, hn_ref, *, n_main):
    j = pl.program_id(1)

    @pl.when(j == 0)
    def _():
        hn = _rms(x_ref[...], g_ref[...]).astype(hn_ref.dtype)
        hn_ref[...] = hn
        dt_ref[...] = jnp.dot(hn, wdt_ref[...], preferred_element_type=F32)

    @pl.when(j < n_main)
    def _():
        o_ref[...] = jnp.dot(hn_ref[...], w_ref[...].astype(BF16), preferred_element_type=F32).astype(o_ref.dtype)

    @pl.when(j >= n_main)
    def _():
        o_ref[...] = jnp.dot(hn_ref[...], wg_ref[...].astype(BF16), preferred_element_type=F32).astype(o_ref.dtype)


def _norm_matmul(x, g, w, layer, w_gates, w_dt, out_dtype, tm, tn):
    t, d = x.shape
    n_main = MAIN_COLS // tn
    n_gate = w_gates.shape[1] // tn
    return pl.pallas_call(
        functools.partial(_norm_matmul_kernel, n_main=n_main),
        grid=(t // tm, n_main + n_gate),
        in_specs=[
            pl.BlockSpec((tm, d), lambda i, j: (i, 0)),
            pl.BlockSpec((1, d), lambda i, j: (0, 0)),
            pl.BlockSpec((None, d, tn), lambda i, j: (layer, 0, jnp.minimum(j, n_main - 1))),
            pl.BlockSpec((d, tn), lambda i, j: (0, jnp.maximum(j - n_main, 0))),
            pl.BlockSpec((d, LANES), lambda i, j: (0, 0)),
        ],
        out_specs=[pl.BlockSpec((tm, tn), lambda i, j: (i, j)), pl.BlockSpec((tm, LANES), lambda i, j: (i, 0))],
        out_shape=[jax.ShapeDtypeStruct((t, (n_main + n_gate) * tn), out_dtype),
                   jax.ShapeDtypeStruct((t, LANES), F32)],
        scratch_shapes=[pltpu.VMEM((tm, d), BF16)],
        compiler_params=_params("parallel", "arbitrary"),
        name="in_proj",
    )(x, g, w, w_gates, w_dt)


def _dt_prep_kernel(raw_ref, bias_ref, a_ref, ac_ref, act_ref, dtt_ref):
    raw = raw_ref[...] + bias_ref[...]
    dt = jnp.maximum(raw, 0.0) + jnp.log1p(jnp.exp(-jnp.abs(raw)))
    a = dt * (a_ref[...] * LOG2E)
    s = a.shape[0]
    pos = lax.broadcasted_iota(jnp.int32, a.shape, 0) & (SSM_CHUNK - 1)
    pre = a
    suf = a
    k = 1
    while k < SSM_CHUNK:
        pre = pre + jnp.where(pos >= k, pltpu.roll(pre, k, 0), 0.0)
        suf = suf + jnp.where(pos < SSM_CHUNK - k, pltpu.roll(suf, s - k, 0), 0.0)
        k *= 2
    lane = lax.broadcasted_iota(jnp.int32, a.shape, 1)
    is_fwd = (lane & (DT_LANES_PER_GROUP - 1)) < SSM_HEADS_PER_GROUP
    ac = jnp.where(is_fwd, pre, suf)
    act_ref[...] = ac.T
    dtt_ref[...] = dt.T
    for grp in range(SSM_GROUPS):
        shift = (LANES - grp * DT_LANES_PER_GROUP) % LANES
        ac_ref[:, grp * LANES:(grp + 1) * LANES] = ac if shift == 0 else pltpu.roll(ac, shift, 1)


def _dt_prep(dt_raw3, bias, a_neg):
    b, s, _ = dt_raw3.shape
    out = jax.ShapeDtypeStruct((b, s, SSM_GROUPS * LANES), F32)
    out_t = jax.ShapeDtypeStruct((b, LANES, s), F32)
    vec = pl.BlockSpec((1, LANES), lambda i: (0, 0))
    return pl.pallas_call(
        _dt_prep_kernel,
        grid=(b,),
        in_specs=[
            pl.BlockSpec((None, s, LANES), lambda i: (i, 0, 0)),
            vec,
            vec,
        ],
        out_specs=[
            pl.BlockSpec((None, s, SSM_GROUPS * LANES), lambda i: (i, 0, 0)),
            pl.BlockSpec((None, LANES, s), lambda i: (i, 0, 0)),
            pl.BlockSpec((None, LANES, s), lambda i: (i, 0, 0)),
        ],
        out_shape=[out, out_t, out_t],
        compiler_params=_params("parallel"),
        name="dt_prep",
    )(dt_raw3, bias, a_neg)


def _conv_kernel(x_ref, w_ref, b_ref, o_ref):
    xf = x_ref[...].astype(F32)
    s, tc = xf.shape
    pad = SSM_CONV // 2
    xe = jnp.concatenate([xf, jnp.zeros((8, tc), F32)], axis=0)
    acc = xf * w_ref[pad:pad + 1, :] + b_ref[...]
    for j in range(SSM_CONV):
        shift = pad - j
        if shift != 0:
            acc = acc + pltpu.roll(xe, shift % (s + 8), 0)[:s, :] * w_ref[j:j + 1, :]
    o_ref[...] = _silu(acc).astype(o_ref.dtype)


def _conv_silu(proj3, conv_w, conv_b, tc):
    b, s, _ = proj3.shape
    col0 = COL_XBC // tc
    return pl.pallas_call(
        _conv_kernel,
        grid=(b, SSM_CONV_CH // tc),
        in_specs=[
            pl.BlockSpec((None, s, tc), lambda i, c: (i, 0, col0 + c)),
            pl.BlockSpec((SSM_CONV, tc), lambda i, c: (0, c)),
            pl.BlockSpec((1, tc), lambda i, c: (0, c)),
        ],
        out_specs=pl.BlockSpec((None, s, tc), lambda i, c: (i, 0, c)),
        out_shape=jax.ShapeDtypeStruct((b, s, SSM_CONV_CH), BF16),
        compiler_params=_params("parallel", "parallel"),
        name="conv_silu",
    )(proj3, conv_w, conv_b)


def _rel_bucket(rel):
    half = NUM_BUCKETS // 2
    max_exact = half // 2
    ret = jnp.where(rel > 0, half, 0)
    n = jnp.abs(rel)
    n_safe = jnp.maximum(n, 1).astype(F32)
    large = max_exact + (jnp.log(n_safe / max_exact) / math.log(MAX_DISTANCE / max_exact)
                         * (half - max_exact)).astype(jnp.int32)
    large = jnp.minimum(large, half - 1)
    return ret + jnp.where(n < max_exact, n, large)


def _bias_kernel(relb_ref, bid_ref, o_ref, *, tq):
    h = pl.program_id(0)
    bid = bid_ref[...]
    fv = jnp.zeros(bid.shape, F32)
    for bkt in range(NUM_BUCKETS):
        fv = jnp.where(bid == bkt, relb_ref[h, bkt], fv)
    base = min(tq, 2 * LANES)
    g = jnp.broadcast_to(fv, (base, bid.shape[1]))
    row = lax.broadcasted_iota(jnp.int32, g.shape, 0)
    k = 1
    while k < base:
        g = jnp.where((row & k) != 0, pltpu.roll(g, k, 1), g)
        k *= 2
    for r0 in range(0, tq, base):
        o_ref[r0:r0 + base, :] = g if r0 == 0 else pltpu.roll(g, r0, 1)


def _bias_windows(rel_bias, s, tq):
    width = 2 * s
    m = jnp.arange(width, dtype=jnp.int32)
    bid = _rel_bucket(((m + tq) % width) - s).astype(jnp.int32)[None, :]
    return pl.pallas_call(
        functools.partial(_bias_kernel, tq=tq),
        grid=(ATTN_HEADS,),
        in_specs=[
            pl.BlockSpec(memory_space=pltpu.SMEM),
            pl.BlockSpec((1, width), lambda h: (0, 0)),
        ],
        out_specs=pl.BlockSpec((None, tq, width), lambda h: (h, 0, 0)),
        out_shape=jax.ShapeDtypeStruct((ATTN_HEADS, tq, width), F32),
        compiler_params=_params("parallel"),
        name="rel_bias_windows",
    )(rel_bias.T.astype(F32), bid)


SOFTMAX_SHIFT_SLACK = 40.0
BF16_ROUNDING_SLACK = 1.01


def _attn_kernel(relb_ref, q_ref, k_ref, v_ref, bias_ref, gqk_ref, lamqk_ref, gh_ref, o_ref, q0_ref, q1_ref, kn_ref,
                 vx_ref, scal_ref, *, tq, lam_init):
    h = pl.program_id(0)
    qi = pl.program_id(2)
    s = k_ref.shape[0]
    lo = lax.broadcasted_iota(jnp.int32, (1, LANES), 1) < ATTN_HEAD_DIM

    def half_sums(x2):
        s_all = jnp.sum(x2, axis=-1, keepdims=True)
        s_lo = jnp.sum(jnp.where(lo, x2, 0.0), axis=-1, keepdims=True)
        return s_lo, jnp.maximum(s_all - s_lo, 0.0)

    def half_norm(x, g):
        s_lo, s_hi = half_sums(x * x)
        ms = jnp.where(lo, s_lo, s_hi) * (1.0 / ATTN_HEAD_DIM)
        return x * lax.rsqrt(ms + RMS_EPS) * g

    @pl.when(qi == 0)
    def _():
        kn_ref[...] = half_norm(k_ref[...].astype(F32), gqk_ref[1:2, :]).astype(BF16)
        vx_ref[:, :LANES] = v_ref[...]
        vx_ref[:, LANES:] = jnp.ones((s, LANES), BF16)
        bmax = relb_ref[h, 0]
        bmin = relb_ref[h, 0]
        for bkt in range(1, NUM_BUCKETS):
            bmax = jnp.maximum(bmax, relb_ref[h, bkt])
            bmin = jnp.minimum(bmin, relb_ref[h, bkt])
        qn = (half_norm(q_ref[...].astype(F32), gqk_ref[0:1, :]) * (ATTN_HEAD_DIM ** -0.5)).astype(BF16)
        q0_ref[...] = jnp.where(lo, qn, jnp.zeros_like(qn))
        q1_ref[...] = jnp.where(lo, jnp.zeros_like(qn), qn)
        gains = jnp.abs(gqk_ref[...])
        reach = (BF16_ROUNDING_SLACK * ATTN_HEAD_DIM ** 0.5) * jnp.max(gains[0:1, :]) * jnp.max(gains[1:2, :])
        scal_ref[0] = reach + bmax
        scal_ref[1] = 2.0 * reach + (bmax - bmin)

    rows = pl.ds(pl.multiple_of(qi * tq, tq), tq)
    q0 = q0_ref[rows, :]
    q1 = q1_ref[rows, :]
    shift = scal_ref[0]
    slack = scal_ref[1]
    c0 = pl.multiple_of(s - (qi + 1) * tq, tq)
    lq = lamqk_ref[...]
    lam = (jnp.exp(jnp.sum(lq[0:1] * lq[1:2], axis=-1, keepdims=True))
           - jnp.exp(jnp.sum(lq[2:3] * lq[3:4], axis=-1, keepdims=True)) + lam_init)

    def logits_of(qm):
        return lax.dot_general(qm, kn_ref[...], (((1,), (1,)), ((), ())),
                               preferred_element_type=F32) + bias_ref[:, pl.ds(c0, s)]

    def finish(out):
        o_ref[...] = (_rms(out, gh_ref[...]) * (1.0 - lam_init)).astype(o_ref.dtype)

    @pl.when(slack <= SOFTMAX_SHIFT_SLACK)
    def _():
        def weighted_v(qm):
            p = jnp.exp(logits_of(qm) - shift).astype(BF16)
            return jnp.dot(p, vx_ref[...], preferred_element_type=F32)

        o0 = weighted_v(q0)
        o1 = weighted_v(q1)
        finish(o0[:, :LANES] / o0[:, LANES:LANES + 1] - o1[:, :LANES] * (lam / o1[:, LANES:LANES + 1]))

    @pl.when(slack > SOFTMAX_SHIFT_SLACK)
    def _():
        def softmax_map(qm):
            logits = logits_of(qm)
            p = jnp.exp(logits - jnp.max(logits, axis=-1, keepdims=True))
            return p, jnp.sum(p, axis=-1, keepdims=True)

        p0, l0 = softmax_map(q0)
        p1, l1 = softmax_map(q1)
        attn = p0 * (1.0 / l0) - p1 * (lam / l1)
        finish(jnp.dot(attn.astype(BF16), v_ref[...], preferred_element_type=F32))


def _diff_attention(proj3, bias_win, relb_t, gqk, lamqk, gh, tq, lam_init):
    b, s, _ = proj3.shape
    width = bias_win.shape[-1]
    return pl.pallas_call(
        functools.partial(_attn_kernel, tq=tq, lam_init=lam_init),
        grid=(ATTN_HEADS, b, s // tq),
        in_specs=[
            pl.BlockSpec(memory_space=pltpu.SMEM),
            pl.BlockSpec((None, s, LANES), lambda h, i, q: (i, 0, COL_Q // LANES + h)),
            pl.BlockSpec((None, s, LANES), lambda h, i, q: (i, 0, COL_K // LANES + h)),
            pl.BlockSpec((None, s, LANES), lambda h, i, q: (i, 0, COL_V // LANES + h)),
            pl.BlockSpec((None, tq, width), lambda h, i, q: (h, 0, 0)),
            pl.BlockSpec((2, LANES), lambda h, i, q: (0, 0)),
            pl.BlockSpec((4, ATTN_HEAD_DIM), lambda h, i, q: (0, 0)),
            pl.BlockSpec((1, LANES), lambda h, i, q: (0, 0)),
        ],
        out_specs=pl.BlockSpec((None, tq, LANES), lambda h, i, q: (i, q, h)),
        out_shape=jax.ShapeDtypeStruct((b, s, ATTN_WIDTH), BF16),
        scratch_shapes=[pltpu.VMEM((s, LANES), BF16), pltpu.VMEM((s, LANES), BF16), pltpu.VMEM((s, LANES), BF16),
                        pltpu.VMEM((s, 2 * LANES), BF16), pltpu.SMEM((2,), F32)],
        compiler_params=_params("parallel", "parallel", "arbitrary"),
        name="diff_attention",
    )(relb_t, proj3, proj3, proj3, bias_win, gqk, lamqk, gh)


def _ssd_kernel(x_ref, b_ref, c_ref, ac_ref, act_ref, dtt_ref, dsk_ref, y_ref, st_ref, acc_ref, bt_ref):
    L = SSM_CHUNK
    s = x_ref.shape[0]
    n_chunks = s // L
    row = lax.broadcasted_iota(jnp.int32, (L, L), 0)
    col = lax.broadcasted_iota(jnp.int32, (L, L), 1)
    lo = lax.broadcasted_iota(jnp.int32, (L, LANES), 1) < SSM_HEAD_DIM
    lo1 = lo[0:1, :]
    n_pairs = SSM_HEADS_PER_GROUP // 2

    bt_ref[...] = b_ref[...].astype(F32).T.astype(BF16)

    def one_chunk(c, backward):
        d = 1 if backward else 0
        rows = pl.ds(pl.multiple_of(c * L, L), L)
        off = SSM_HEADS_PER_GROUP * d
        tot_row = 0 if backward else L - 1
        mask = (row <= col) if backward else (row >= col)
        xb = x_ref[rows, :]
        cc = c_ref[rows, :]
        acc = ac_ref[rows, :]
        act = act_ref[:, rows]
        dtt = dtt_ref[:, rows]
        bt = bt_ref[:, rows].astype(F32)
        cb = lax.dot_general(cc, b_ref[rows, :], (((1,), (1,)), ((), ())), preferred_element_type=F32)

        ys = []
        for j in range(n_pairs):
            lanes = slice(j * LANES, (j + 1) * LANES)
            x_pair = xb[:, lanes]
            y_off = jnp.dot(cc, st_ref[d, :, lanes].astype(BF16), preferred_element_type=F32)
            diag, stn, acol, tot = [], [], [], []
            for e in (2 * j, 2 * j + 1):
                a_col = jnp.broadcast_to(acc[:, off + e:off + e + 1], (L, LANES))
                a_row = act[off + e:off + e + 1, :]
                dt_row = dtt[off + e:off + e + 1, :]
                tot_e = acc[tot_row:tot_row + 1, off + e:off + e + 1]
                decay = jnp.where(mask, jnp.exp2(a_col - a_row), 0.0)
                diag.append(jnp.dot((cb * decay * dt_row).astype(BF16), x_pair, preferred_element_type=F32))
                w_row = dt_row * jnp.exp2(tot_e - a_row)
                stn.append(jnp.dot((bt * w_row).astype(BF16), x_pair, preferred_element_type=F32))
                acol.append(a_col)
                tot.append(tot_e)
            ys.append(jnp.where(lo, diag[0], diag[1]) + y_off * jnp.exp2(jnp.where(lo, acol[0], acol[1])))
            st_ref[d, :, lanes] = (st_ref[d, :, lanes] * jnp.exp2(jnp.where(lo1, tot[0], tot[1]))
                                   + jnp.where(lo, stn[0], stn[1]))
        y = jnp.concatenate(ys, axis=1)
        if not backward:
            y = y + xb.astype(F32) * dsk_ref[...]
        acc_ref[d, rows, :] = y

    st_ref[...] = jnp.zeros_like(st_ref)

    def body(c, carry):
        one_chunk(c, False)
        one_chunk(n_chunks - 1 - c, True)
        return carry

    lax.fori_loop(0, n_chunks, body, 0)
    y_ref[...] = (acc_ref[0] + acc_ref[1]).astype(y_ref.dtype)


def _ssd_scan(u, ac_g, act_g, dtt_g, dskip):
    b, s, _ = u.shape
    gw = SSM_HEADS_PER_GROUP * SSM_HEAD_DIM
    xcols = SSM_INNER // SSM_STATE
    nl = DT_LANES_PER_GROUP
    return pl.pallas_call(
        _ssd_kernel,
        grid=(b, SSM_GROUPS),
        in_specs=[
            pl.BlockSpec((None, s, gw), lambda i, g: (i, 0, g)),
            pl.BlockSpec((None, s, SSM_STATE), lambda i, g: (i, 0, xcols + g)),
            pl.BlockSpec((None, s, SSM_STATE), lambda i, g: (i, 0, xcols + SSM_GROUPS + g)),
            pl.BlockSpec((None, s, LANES), lambda i, g: (i, 0, g)),
            pl.BlockSpec((None, nl, s), lambda i, g: (i, g, 0)),
            pl.BlockSpec((None, nl, s), lambda i, g: (i, g, 0)),
            pl.BlockSpec((1, gw), lambda i, g: (0, g)),
        ],
        out_specs=pl.BlockSpec((None, s, gw), lambda i, g: (i, 0, g)),
        out_shape=jax.ShapeDtypeStruct((b, s, SSM_INNER), BF16),
        scratch_shapes=[pltpu.VMEM((2, SSM_STATE, gw), F32), pltpu.VMEM((2, s, gw), F32),
                        pltpu.VMEM((SSM_STATE, s), BF16)],
        compiler_params=_params("parallel", "parallel"),
        name="ssd_scan",
    )(u, u, u, ac_g, act_g, dtt_g, dskip)


def _route(logits):
    lane = lax.broadcasted_iota(jnp.int32, logits.shape, 1)
    lane_f = lane.astype(F32)
    neg = -jnp.inf
    big = float(LANES)
    is_g = lane < MOE_GROUPS
    gl = jnp.where(is_g, logits, neg)
    gmax = jnp.max(gl, axis=-1, keepdims=True)
    g_idx = jnp.min(jnp.where(gl == gmax, lane_f, big), axis=-1, keepdims=True)
    g_gate = 1.0 / jnp.sum(jnp.where(is_g, jnp.exp(logits - gmax), 0.0), axis=-1, keepdims=True)
    first = MOE_GROUPS + MOE_EXPERTS_PER_GROUP * g_idx
    in_grp = (lane_f >= first) & (lane_f < first + MOE_EXPERTS_PER_GROUP)
    el = jnp.where(in_grp, logits, neg)
    v1 = jnp.max(el, axis=-1, keepdims=True)
    i1 = jnp.min(jnp.where(el == v1, lane_f, big), axis=-1, keepdims=True)
    el2 = jnp.where(lane_f == i1, neg, el)
    v2 = jnp.max(el2, axis=-1, keepdims=True)
    i2 = jnp.min(jnp.where(el2 == v2, lane_f, big), axis=-1, keepdims=True)
    r = jnp.exp(v2 - v1)
    p1 = 1.0 / (1.0 + r)
    p2 = r * p1
    ids = jnp.where(lane == 0, i1 - MOE_GROUPS, jnp.where(lane == 1, i2 - MOE_GROUPS, 0.0)).astype(jnp.int32)
    gates = jnp.where(lane == 0, p1 * g_gate, jnp.where(lane == 1, p2 * g_gate, 0.0))
    return ids, gates


_HI_HALF = 0xFFFF0000


def _pack_bf16_pairs(x):
    n = x.shape[1] // 2
    bits = lax.bitcast_convert_type(x.astype(BF16).astype(F32), jnp.uint32)
    return (bits[:, n:] & jnp.uint32(_HI_HALF)) | (bits[:, :n] >> 16)


def _unpack_bf16_pairs(w):
    lo = lax.bitcast_convert_type(w << 16, F32)
    hi = lax.bitcast_convert_type(w & jnp.uint32(_HI_HALF), F32)
    return jnp.concatenate([lo, hi], axis=1)


def _rank_assignments(ids, carry):
    tt = ids.shape[0]
    lane = lax.broadcasted_iota(jnp.int32, ids.shape, 1)
    oh0 = jnp.where(lane == ids[:, 0:1], 1.0, 0.0)
    oh1 = jnp.where(lane == ids[:, 1:2], 1.0, 0.0)
    earlier = jnp.where(lax.broadcasted_iota(jnp.int32, (tt, tt), 1) < lax.broadcasted_iota(jnp.int32, (tt, tt), 0),
                        1.0, 0.0).astype(BF16)
    before0 = jnp.dot(earlier, oh0.astype(BF16), preferred_element_type=F32)
    before1 = jnp.dot(earlier, oh1.astype(BF16), preferred_element_type=F32)
    tot0 = jnp.sum(oh0, axis=0, keepdims=True)
    rank0 = jnp.sum(oh0 * (before0 + carry), axis=-1, keepdims=True)
    rank1 = jnp.sum(oh1 * (before1 + tot0 + carry), axis=-1, keepdims=True)
    rank = jnp.where(lane == 0, rank0, jnp.where(lane == 1, rank1, 0.0)).astype(jnp.int32)
    return rank, carry + tot0 + jnp.sum(oh1, axis=0, keepdims=True)


def _post_kernel(attn_ref, y_ref, z0_ref, z1_ref, ga_ref, gm_ref, x_ref, gs_ref, wa_ref, ws_ref, wo_ref, g2_ref,
                 wr_ref, br_ref, xo_ref, hn_ref, id_ref, gate_ref, rank_ref, cnt_ref, carry_ref):
    @pl.when(pl.program_id(0) == 0)
    def _():
        carry_ref[...] = jnp.zeros_like(carry_ref)

    ya = jnp.dot(attn_ref[...], wa_ref[...], preferred_element_type=F32)
    z = jnp.concatenate([z0_ref[...], z1_ref[...]], axis=1).astype(F32)
    gated = y_ref[...].astype(F32) * _silu(z)
    ym = jnp.dot(_rms(gated, gs_ref[...]).astype(BF16), ws_ref[...], preferred_element_type=F32)
    merged = (_sigmoid(ga_ref[...].astype(F32)) * ya + _sigmoid(gm_ref[...].astype(F32)) * ym)
    x_new = x_ref[...] + jnp.dot(merged.astype(BF16), wo_ref[...], preferred_element_type=F32)
    xo_ref[...] = x_new
    hn = _rms(x_new, g2_ref[...])
    hn_ref[:, 0, :] = _pack_bf16_pairs(hn)
    hn_hi = hn.astype(BF16)
    hn_lo = (hn - hn_hi.astype(F32)).astype(BF16)
    wr = wr_ref[...]
    wr_hi = wr.astype(BF16)
    wr_lo = (wr - wr_hi.astype(F32)).astype(BF16)
    logits = (jnp.dot(hn_hi, wr_hi, preferred_element_type=F32) + jnp.dot(hn_lo, wr_hi, preferred_element_type=F32)
              + jnp.dot(hn_hi, wr_lo, preferred_element_type=F32) + br_ref[...])
    ids, gates = _route(logits)
    id_ref[...] = ids
    gate_ref[...] = gates
    rank, carry = _rank_assignments(ids, carry_ref[...])
    rank_ref[...] = rank
    carry_ref[...] = carry
    cnt_ref[...] = carry


def _post_mixer(attn, y, proj, x, gs, wa, ws, wo, g2, wr, br, tm):
    t, d = x.shape
    row = lambda w, c: pl.BlockSpec((tm, w), lambda i: (i, c))
    full = lambda a: pl.BlockSpec(a.shape, lambda i: (0,) * a.ndim)
    return pl.pallas_call(
        _post_kernel,
        grid=(t // tm,),
        in_specs=[
            row(ATTN_WIDTH, 0),
            row(SSM_INNER, 0),
            row(D_MODEL, COL_Z // D_MODEL),
            row(D_MODEL, COL_Z // D_MODEL + 1),
            row(D_MODEL, COL_GA // D_MODEL),
            row(D_MODEL, COL_GM // D_MODEL),
            row(D_MODEL, 0),
            full(gs), full(wa), full(ws), full(wo), full(g2), full(wr), full(br),
        ],
        out_specs=[row(D_MODEL, 0), pl.BlockSpec((tm, 1, PACKED), lambda i: (i, 0, 0)), row(LANES, 0), row(LANES, 0),
                   row(LANES, 0), pl.BlockSpec((1, LANES), lambda i: (0, 0))],
        out_shape=[
            jax.ShapeDtypeStruct((t, d), F32),
            jax.ShapeDtypeStruct((t, 1, PACKED), jnp.uint32),
            jax.ShapeDtypeStruct((t, LANES), jnp.int32),
            jax.ShapeDtypeStruct((t, LANES), F32),
            jax.ShapeDtypeStruct((t, LANES), jnp.int32),
            jax.ShapeDtypeStruct((1, LANES), F32),
        ],
        scratch_shapes=[pltpu.VMEM((1, LANES), F32)],
        compiler_params=_params("arbitrary"),
        name="post_mixer",
    )(attn, y, proj, proj, proj, proj, x, gs, wa, ws, wo, g2, wr, br)


GATHER_DEPTH = 3


def _start_rows(idx_ref, base, src_hbm, dst_ref, sem, rows=None):
    for r in (range(dst_ref.shape[0]) if rows is None else rows):
        pltpu.make_async_copy(src_hbm.at[idx_ref[base + r]], dst_ref.at[pl.ds(r, 1), :], sem).start(priority=r % 2)


def _wait_buffer(dst_ref, sem):
    pltpu.make_async_copy(dst_ref, dst_ref, sem).wait()


def _moe_kernel(blk_e_ref, pos_ref, nused_ref, seg_ref, hn_hbm, wg_ref, wu_ref, wd_ref, y_ref, src_ref, xg_ref,
                wgb_ref, wub_ref, wdb_ref, sem, *, n_tokens):
    i = pl.program_id(0)
    nused = nused_ref[0]
    quarter = EXPERT_ROWS // 4

    def start(blk, part=None):
        slot = blk % GATHER_DEPTH
        rows = range(EXPERT_ROWS) if part is None else range(part * quarter, (part + 1) * quarter)
        _start_rows(src_ref, blk * EXPERT_ROWS, hn_hbm, xg_ref.at[slot], sem.at[slot], rows)

    @pl.when(i == 0)
    def _():
        def clear(j, carry):
            src_ref[j] = 0
            return carry

        for e in range(MOE_EXPERTS):
            lax.fori_loop(seg_ref[e], seg_ref[MOE_EXPERTS + e], clear, 0)
        for k in range(MOE_TOPK):
            def fill(tok, carry):
                src_ref[pos_ref[k * n_tokens + tok]] = tok
                return carry

            lax.fori_loop(0, n_tokens, fill, 0, unroll=8)
        start(0)

    @pl.when((i == 0) & (nused > 1))
    def _():
        start(1)

    new_expert = (i == 0) | (blk_e_ref[i] != blk_e_ref[jnp.maximum(i - 1, 0)])

    @pl.when((i < nused) & new_expert)
    def _():
        wgb_ref[...] = wg_ref[...].astype(BF16)
        wub_ref[...] = wu_ref[...].astype(BF16)
        wdb_ref[...] = wd_ref[...].astype(BF16)

    def compute(prefetch):
        xb = _unpack_bf16_pairs(xg_ref[i % GATHER_DEPTH]).astype(BF16)
        prefetch(0)
        hg = jnp.dot(xb, wgb_ref[...], preferred_element_type=F32)
        prefetch(1)
        hu = jnp.dot(xb, wub_ref[...], preferred_element_type=F32)
        prefetch(2)
        hid = (_silu(hg) * hu).astype(BF16)
        y = jnp.dot(hid, wdb_ref[...], preferred_element_type=F32)
        prefetch(3)
        y_ref[:, 0, :] = _pack_bf16_pairs(y)

    def wait():
        slot = i % GATHER_DEPTH
        _wait_buffer(xg_ref.at[slot], sem.at[slot])

    @pl.when(i + 2 < nused)
    def _():
        wait()
        compute(lambda part: start(i + 2, part))

    @pl.when((i < nused) & (i + 2 >= nused))
    def _():
        wait()
        compute(lambda part: None)

    @pl.when(i >= nused)
    def _():
        y_ref[...] = jnp.zeros_like(y_ref)


def _moe_experts(blk_expert, pos_kt, nused, seg, hn3, wg, wu, wd, layer):
    n_blocks = blk_expert.shape[0]
    d = wg.shape[2]
    n_tokens = hn3.shape[0]
    grid_spec = pltpu.PrefetchScalarGridSpec(
        num_scalar_prefetch=4,
        grid=(n_blocks,),
        in_specs=[
            pl.BlockSpec(memory_space=pl.ANY),
            pl.BlockSpec((None, None, d, MOE_FF), lambda i, be, st, nu, sg: (layer, be[i], 0, 0)),
            pl.BlockSpec((None, None, d, MOE_FF), lambda i, be, st, nu, sg: (layer, be[i], 0, 0)),
            pl.BlockSpec((None, None, MOE_FF, d), lambda i, be, st, nu, sg: (layer, be[i], 0, 0)),
        ],
        out_specs=pl.BlockSpec((EXPERT_ROWS, 1, PACKED), lambda i, be, st, nu, sg: (i, 0, 0)),
        scratch_shapes=[
            pltpu.SMEM((n_blocks * EXPERT_ROWS,), jnp.int32),
            pltpu.VMEM((GATHER_DEPTH, EXPERT_ROWS, PACKED), jnp.uint32),
            pltpu.VMEM((d, MOE_FF), BF16),
            pltpu.VMEM((d, MOE_FF), BF16),
            pltpu.VMEM((MOE_FF, d), BF16),
            pltpu.SemaphoreType.DMA((GATHER_DEPTH,)),
        ],
    )
    return pl.pallas_call(
        functools.partial(_moe_kernel, n_tokens=n_tokens),
        grid_spec=grid_spec,
        out_shape=jax.ShapeDtypeStruct((n_blocks * EXPERT_ROWS, 1, PACKED), jnp.uint32),
        compiler_params=_params("arbitrary"),
        name="moe_experts",
    )(blk_expert, pos_kt, nused, seg, hn3, wg, wu, wd)


def _combine_kernel(pos_ref, x_ref, gate_ref, y_hbm, o_ref, yg_ref, sem, *, tt, n_tiles):
    i = pl.program_id(0)

    def start(tile):
        slot = tile % GATHER_DEPTH
        for k in range(MOE_TOPK):
            _start_rows(pos_ref, (k * n_tiles + tile) * tt, y_hbm, yg_ref.at[slot, k], sem.at[slot])

    def finish():
        gates = gate_ref[...]
        slot = i % GATHER_DEPTH
        o_ref[...] = (x_ref[...] + gates[:, 0:1] * _unpack_bf16_pairs(yg_ref[slot, 0])
                      + gates[:, 1:2] * _unpack_bf16_pairs(yg_ref[slot, 1]))

    def wait():
        slot = i % GATHER_DEPTH
        _wait_buffer(yg_ref.at[slot], sem.at[slot])

    @pl.when(i == 0)
    def _():
        start(0)
        if n_tiles > 1:
            start(1)

    @pl.when(i + 2 < n_tiles)
    def _():
        wait()
        start(i + 2)
        finish()

    @pl.when(i + 2 >= n_tiles)
    def _():
        wait()
        finish()


def _moe_combine(pos_kt, x, gates, y3, tt):
    t, d = x.shape
    n_tiles = t // tt
    grid_spec = pltpu.PrefetchScalarGridSpec(
        num_scalar_prefetch=1,
        grid=(n_tiles,),
        in_specs=[
            pl.BlockSpec((tt, d), lambda i, p: (i, 0)),
            pl.BlockSpec((tt, LANES), lambda i, p: (i, 0)),
            pl.BlockSpec(memory_space=pl.ANY),
        ],
        out_specs=pl.BlockSpec((tt, d), lambda i, p: (i, 0)),
        scratch_shapes=[pltpu.VMEM((GATHER_DEPTH, MOE_TOPK, tt, PACKED), jnp.uint32),
                        pltpu.SemaphoreType.DMA((GATHER_DEPTH,))],
    )
    return pl.pallas_call(
        functools.partial(_combine_kernel, tt=tt, n_tiles=n_tiles),
        grid_spec=grid_spec,
        out_shape=jax.ShapeDtypeStruct((t, d), F32),
        compiler_params=_params("arbitrary"),
        name="moe_combine",
    )(pos_kt, x, gates, y3)


def _dispatch_plan(ids, rank, cnt):
    t = ids.shape[0]
    n_assign = t * MOE_TOPK
    counts = cnt[0, :MOE_EXPERTS].astype(jnp.int32)
    padded = (counts + EXPERT_ROWS - 1) // EXPERT_ROWS * EXPERT_ROWS
    pad_end = jnp.cumsum(padded)
    pad_start = pad_end - padded
    e_ids = ids[:, :MOE_TOPK]
    onehot = e_ids[:, :, None] == jnp.arange(MOE_EXPERTS, dtype=jnp.int32)[None, None, :]
    dest = jnp.sum(jnp.where(onehot, pad_start[None, None, :], 0), axis=-1) + rank[:, :MOE_TOPK]
    n_blocks = -(-n_assign // EXPERT_ROWS) + MOE_EXPERTS
    blk_row0 = jnp.arange(n_blocks, dtype=jnp.int32) * EXPERT_ROWS
    blk_expert = jnp.minimum(jnp.sum((pad_end[None, :] <= blk_row0[:, None]).astype(jnp.int32), axis=1),
                             MOE_EXPERTS - 1)
    nused = (pad_end[-1:] // EXPERT_ROWS).astype(jnp.int32)
    pos_kt = dest.T.reshape(-1).astype(jnp.int32)
    seg = jnp.concatenate([pad_start + counts, pad_end]).astype(jnp.int32)
    return blk_expert, nused, pos_kt, seg


def _pick(n, pref):
    return pref if n % pref == 0 else n


def kernel(x, norm1_g, w_in, qk_norm_g, lambda_qk, attn_head_norm_g, rel_bias, conv_w, conv_b, dt_bias, a_log,
           d_skip, ssm_norm_g, w_attn_out, w_ssm_out, w_out, norm2_g, w_router_group, b_router_group,
           w_router_expert, b_router_expert, w_exp_gate, w_exp_up, w_exp_down):
    b, s, d = x.shape
    t = b * s
    tq = _pick(s, 512)
    bias_win = _bias_windows(rel_bias, s, tq)

    lane = np.arange(DT_COLS)
    grp, rem = lane // DT_LANES_PER_GROUP, lane % DT_LANES_PER_GROUP
    direction, e = rem // SSM_HEADS_PER_GROUP, rem % SSM_HEADS_PER_GROUP
    dt_src = direction * SSM_HEADS + grp * SSM_HEADS_PER_GROUP + e
    pad_lanes = LANES - DT_COLS

    xf = x.reshape(t, d)
    for i in range(DEPTH):
        lam_init = 0.8 - 0.6 * math.exp(-0.3 * i)
        g1 = norm1_g[i][None, :]
        w_tail = lax.slice(w_in, (i, 0, MAIN_COLS), (i + 1, d, w_in.shape[2]))
        w_dt = jnp.pad(w_tail[0, :, :DT_COLS][:, dt_src], ((0, 0), (0, pad_lanes))).astype(BF16)
        dt_b = jnp.pad(dt_bias[i].reshape(-1)[dt_src], (0, pad_lanes))[None, :]
        a_neg = jnp.pad(-jnp.exp(a_log[i].astype(F32)).reshape(-1)[dt_src], (0, pad_lanes))[None, :]

        proj, dt_raw = _norm_matmul(xf, g1, w_in, i, w_tail[0, :, DT_COLS:], w_dt, BF16, _pick(t, 2048), 1024)
        proj3 = proj.reshape(b, s, PROJ_COLS)
        ac, act, dtt = _dt_prep(dt_raw.reshape(b, s, LANES), dt_b, a_neg)

        u = _conv_silu(proj3, conv_w[i], conv_b[i][None, :], 256)
        y_ssd = _ssd_scan(u, ac, act, dtt, jnp.repeat(d_skip[i], SSM_HEAD_DIM)[None, :])

        gqk = jnp.tile(qk_norm_g[i], (1, 2))
        attn = _diff_attention(proj3, bias_win, rel_bias.T.astype(F32), gqk, lambda_qk[i],
                               attn_head_norm_g[i][None, :], tq, lam_init)

        w_r = jnp.pad(jnp.concatenate([w_router_group[i], w_router_expert[i]], axis=1),
                      ((0, 0), (0, LANES - MOE_GROUPS - MOE_EXPERTS)))
        b_r = jnp.pad(jnp.concatenate([b_router_group[i], b_router_expert[i]]),
                      (0, LANES - MOE_GROUPS - MOE_EXPERTS))[None, :]
        x_mid, hn, ids, gates, rank, cnt = _post_mixer(
            attn.reshape(t, ATTN_WIDTH), y_ssd.reshape(t, SSM_INNER), proj, xf, ssm_norm_g[i][None, :],
            w_attn_out[i].astype(BF16), w_ssm_out[i].astype(BF16), w_out[i].astype(BF16), norm2_g[i][None, :],
            w_r, b_r, _pick(t, 256))

        blk_expert, nused, pos_kt, seg = _dispatch_plan(ids, rank, cnt)
        y_pad = _moe_experts(blk_expert, pos_kt, nused, seg, hn, w_exp_gate, w_exp_up, w_exp_down, i)
        xf = _moe_combine(pos_kt, x_mid, gates, y_pad, _pick(t, 512))
    return xf.reshape(b, s, d)
```

```python
import functools
import math

import jax
import jax.numpy as jnp
import numpy as np
from jax import lax
from jax.experimental import pallas as pl
from jax.experimental.pallas import tpu as pltpu

D_MODEL = 1024
DEPTH = 2
ATTN_HEADS = 8
ATTN_HEAD_DIM = 64
ATTN_V_DIM = 2 * ATTN_HEAD_DIM
ATTN_WIDTH = ATTN_HEADS * ATTN_V_DIM
NUM_BUCKETS = 32
MAX_DISTANCE = 128
SSM_INNER = 2 * D_MODEL
SSM_HEAD_DIM = 64
SSM_HEADS = SSM_INNER // SSM_HEAD_DIM
SSM_GROUPS = 4
SSM_HEADS_PER_GROUP = SSM_HEADS // SSM_GROUPS
SSM_STATE = 128
SSM_CONV = 5
SSM_CHUNK = 128
SSM_CONV_CH = SSM_INNER + 2 * SSM_GROUPS * SSM_STATE
MOE_GROUPS = 4
MOE_EXPERTS_PER_GROUP = 8
MOE_EXPERTS = MOE_GROUPS * MOE_EXPERTS_PER_GROUP
MOE_TOPK = 2
MOE_FF = 512
EXPERT_ROWS = 128
RMS_EPS = 1e-6

LANES = 128
COL_Q = 0
COL_K = COL_Q + ATTN_WIDTH
COL_V = COL_K + ATTN_WIDTH
COL_Z = COL_V + ATTN_WIDTH
COL_XBC = COL_Z + SSM_INNER
MAIN_COLS = COL_XBC + SSM_CONV_CH
DT_COLS = 2 * SSM_HEADS
COL_GATES = MAIN_COLS + DT_COLS
COL_GA = MAIN_COLS
COL_GM = COL_GA + D_MODEL
PROJ_COLS = COL_GM + D_MODEL
DT_LANES_PER_GROUP = 2 * SSM_HEADS_PER_GROUP
PACKED = D_MODEL // 2

VMEM_LIMIT_BYTES = 56 * 1024 * 1024
LOG2E = 1.4426950408889634

F32 = jnp.float32
BF16 = jnp.bfloat16


def _params(*semantics):
    return pltpu.CompilerParams(dimension_semantics=semantics, vmem_limit_bytes=VMEM_LIMIT_BYTES)


def _sigmoid(x):
    return 0.5 * jnp.tanh(0.5 * x) + 0.5


def _silu(x):
    h = 0.5 * x
    return h + h * jnp.tanh(h)


def _rms(x, g):
    ms = jnp.mean(x * x, axis=-1, keepdims=True)
    return x * lax.rsqrt(ms + RMS_EPS) * g


def _norm_matmul_kernel(x_ref, g_ref, w_ref, wg_ref, wdt_ref, o_ref, dt_ref, hn_ref, *, n_main):
    j = pl.program_id(1)

    @pl.when(j == 0)
    def _():
        hn = _rms(x_ref[...], g_ref[...]).astype(hn_ref.dtype)
        hn_ref[...] = hn
        dt_ref[...] = jnp.dot(hn, wdt_ref[...], preferred_element_type=F32)

    @pl.when(j < n_main)
    def _():
        o_ref[...] = jnp.dot(hn_ref[...], w_ref[...].astype(BF16), preferred_element_type=F32).astype(o_ref.dtype)

    @pl.when(j >= n_main)
    def _():
        o_ref[...] = jnp.dot(hn_ref[...], wg_ref[...].astype(BF16), preferred_element_type=F32).astype(o_ref.dtype)


def _norm_matmul(x, g, w, layer, w_gates, w_dt, out_dtype, tm, tn):
    t, d = x.shape
    n_main = MAIN_COLS // tn
    n_gate = w_gates.shape[1] // tn
    return pl.pallas_call(
        functools.partial(_norm_matmul_kernel, n_main=n_main),
        grid=(t // tm, n_main + n_gate),
        in_specs=[
            pl.BlockSpec((tm, d), lambda i, j: (i, 0)),
            pl.BlockSpec((1, d), lambda i, j: (0, 0)),
            pl.BlockSpec((None, d, tn), lambda i, j: (layer, 0, jnp.minimum(j, n_main - 1))),
            pl.BlockSpec((d, tn), lambda i, j: (0, jnp.maximum(j - n_main, 0))),
            pl.BlockSpec((d, LANES), lambda i, j: (0, 0)),
        ],
        out_specs=[pl.BlockSpec((tm, tn), lambda i, j: (i, j)), pl.BlockSpec((tm, LANES), lambda i, j: (i, 0))],
        out_shape=[jax.ShapeDtypeStruct((t, (n_main + n_gate) * tn), out_dtype),
                   jax.ShapeDtypeStruct((t, LANES), F32)],
        scratch_shapes=[pltpu.VMEM((tm, d), BF16)],
        compiler_params=_params("parallel", "arbitrary"),
        name="in_proj",
    )(x, g, w, w_gates, w_dt)


def _dt_prep_kernel(raw_ref, bias_ref, a_ref, ac_ref, act_ref, dtt_ref):
    raw = raw_ref[...] + bias_ref[...]
    dt = jnp.maximum(raw, 0.0) + jnp.log1p(jnp.exp(-jnp.abs(raw)))
    a = dt * (a_ref[...] * LOG2E)
    s = a.shape[0]
    pos = lax.broadcasted_iota(jnp.int32, a.shape, 0) & (SSM_CHUNK - 1)
    pre = a
    suf = a
    k = 1
    while k < SSM_CHUNK:
        pre = pre + jnp.where(pos >= k, pltpu.roll(pre, k, 0), 0.0)
        suf = suf + jnp.where(pos < SSM_CHUNK - k, pltpu.roll(suf, s - k, 0), 0.0)
        k *= 2
    lane = lax.broadcasted_iota(jnp.int32, a.shape, 1)
    is_fwd = (lane & (DT_LANES_PER_GROUP - 1)) < SSM_HEADS_PER_GROUP
    ac = jnp.where(is_fwd, pre, suf)
    act_ref[...] = ac.T
    dtt_ref[...] = dt.T
    for grp in range(SSM_GROUPS):
        shift = (LANES - grp * DT_LANES_PER_GROUP) % LANES
        ac_ref[:, grp * LANES:(grp + 1) * LANES] = ac if shift == 0 else pltpu.roll(ac, shift, 1)


def _dt_prep(dt_raw3, bias, a_neg):
    b, s, _ = dt_raw3.shape
    out = jax.ShapeDtypeStruct((b, s, SSM_GROUPS * LANES), F32)
    out_t = jax.ShapeDtypeStruct((b, LANES, s), F32)
    vec = pl.BlockSpec((1, LANES), lambda i: (0, 0))
    return pl.pallas_call(
        _dt_prep_kernel,
        grid=(b,),
        in_specs=[
            pl.BlockSpec((None, s, LANES), lambda i: (i, 0, 0)),
            vec,
            vec,
        ],
        out_specs=[
            pl.BlockSpec((None, s, SSM_GROUPS * LANES), lambda i: (i, 0, 0)),
            pl.BlockSpec((None, LANES, s), lambda i: (i, 0, 0)),
            pl.BlockSpec((None, LANES, s), lambda i: (i, 0, 0)),
        ],
        out_shape=[out, out_t, out_t],
        compiler_params=_params("parallel"),
        name="dt_prep",
    )(dt_raw3, bias, a_neg)


def _conv_kernel(x_ref, w_ref, b_ref, o_ref):
    xf = x_ref[...].astype(F32)
    s, tc = xf.shape
    pad = SSM_CONV // 2
    xe = jnp.concatenate([xf, jnp.zeros((8, tc), F32)], axis=0)
    acc = xf * w_ref[pad:pad + 1, :] + b_ref[...]
    for j in range(SSM_CONV):
        shift = pad - j
        if shift != 0:
            acc = acc + pltpu.roll(xe, shift % (s + 8), 0)[:s, :] * w_ref[j:j + 1, :]
    o_ref[...] = _silu(acc).astype(o_ref.dtype)


def _conv_silu(proj3, conv_w, conv_b, tc):
    b, s, _ = proj3.shape
    col0 = COL_XBC // tc
    return pl.pallas_call(
        _conv_kernel,
        grid=(b, SSM_CONV_CH // tc),
        in_specs=[
            pl.BlockSpec((None, s, tc), lambda i, c: (i, 0, col0 + c)),
            pl.BlockSpec((SSM_CONV, tc), lambda i, c: (0, c)),
            pl.BlockSpec((1, tc), lambda i, c: (0, c)),
        ],
        out_specs=pl.BlockSpec((None, s, tc), lambda i, c: (i, 0, c)),
        out_shape=jax.ShapeDtypeStruct((b, s, SSM_CONV_CH), BF16),
        compiler_params=_params("parallel", "parallel"),
        name="conv_silu",
    )(proj3, conv_w, conv_b)


def _rel_bucket(rel):
    half = NUM_BUCKETS // 2
    max_exact = half // 2
    ret = jnp.where(rel > 0, half, 0)
    n = jnp.abs(rel)
    n_safe = jnp.maximum(n, 1).astype(F32)
    large = max_exact + (jnp.log(n_safe / max_exact) / math.log(MAX_DISTANCE / max_exact)
                         * (half - max_exact)).astype(jnp.int32)
    large = jnp.minimum(large, half - 1)
    return ret + jnp.where(n < max_exact, n, large)


def _bias_kernel(relb_ref, bid_ref, o_ref, *, tq):
    h = pl.program_id(0)
    bid = bid_ref[...]
    fv = jnp.zeros(bid.shape, F32)
    for bkt in range(NUM_BUCKETS):
        fv = jnp.where(bid == bkt, relb_ref[h, bkt], fv)
    base = min(tq, 2 * LANES)
    g = jnp.broadcast_to(fv, (base, bid.shape[1]))
    row = lax.broadcasted_iota(jnp.int32, g.shape, 0)
    k = 1
    while k < base:
        g = jnp.where((row & k) != 0, pltpu.roll(g, k, 1), g)
        k *= 2
    for r0 in range(0, tq, base):
        o_ref[r0:r0 + base, :] = g if r0 == 0 else pltpu.roll(g, r0, 1)


def _bias_windows(rel_bias, s, tq):
    width = 2 * s
    m = jnp.arange(width, dtype=jnp.int32)
    bid = _rel_bucket(((m + tq) % width) - s).astype(jnp.int32)[None, :]
    return pl.pallas_call(
        functools.partial(_bias_kernel, tq=tq),
        grid=(ATTN_HEADS,),
        in_specs=[
            pl.BlockSpec(memory_space=pltpu.SMEM),
            pl.BlockSpec((1, width), lambda h: (0, 0)),
        ],
        out_specs=pl.BlockSpec((None, tq, width), lambda h: (h, 0, 0)),
        out_shape=jax.ShapeDtypeStruct((ATTN_HEADS, tq, width), F32),
        compiler_params=_params("parallel"),
        name="rel_bias_windows",
    )(rel_bias.T.astype(F32), bid)


SOFTMAX_SHIFT_SLACK = 40.0
BF16_ROUNDING_SLACK = 1.01


def _attn_kernel(relb_ref, q_ref, k_ref, v_ref, bias_ref, gqk_ref, lamqk_ref, gh_ref, o_ref, q0_ref, q1_ref, kn_ref,
                 vx_ref, scal_ref, *, tq, lam_init):
    h = pl.program_id(0)
    qi = pl.program_id(2)
    s = k_ref.shape[0]
    lo = lax.broadcasted_iota(jnp.int32, (1, LANES), 1) < ATTN_HEAD_DIM

    def half_sums(x2):
        s_all = jnp.sum(x2, axis=-1, keepdims=True)
        s_lo = jnp.sum(jnp.where(lo, x2, 0.0), axis=-1, keepdims=True)
        return s_lo, jnp.maximum(s_all - s_lo, 0.0)

    def half_norm(x, g):
        s_lo, s_hi = half_sums(x * x)
        ms = jnp.where(lo, s_lo, s_hi) * (1.0 / ATTN_HEAD_DIM)
        return x * lax.rsqrt(ms + RMS_EPS) * g

    @pl.when(qi == 0)
    def _():
        kn_ref[...] = half_norm(k_ref[...].astype(F32), gqk_ref[1:2, :]).astype(BF16)
        vx_ref[:, :LANES] = v_ref[...]
        vx_ref[:, LANES:] = jnp.ones((s, LANES), BF16)
        bmax = relb_ref[h, 0]
        bmin = relb_ref[h, 0]
        for bkt in range(1, NUM_BUCKETS):
            bmax = jnp.maximum(bmax, relb_ref[h, bkt])
            bmin = jnp.minimum(bmin, relb_ref[h, bkt])
        qn = (half_norm(q_ref[...].astype(F32), gqk_ref[0:1, :]) * (ATTN_HEAD_DIM ** -0.5)).astype(BF16)
        q0_ref[...] = jnp.where(lo, qn, jnp.zeros_like(qn))
        q1_ref[...] = jnp.where(lo, jnp.zeros_like(qn), qn)
        gains = jnp.abs(gqk_ref[...])
        reach = (BF16_ROUNDING_SLACK * ATTN_HEAD_DIM ** 0.5) * jnp.max(gains[0:1, :]) * jnp.max(gains[1:2, :])
        scal_ref[0] = reach + bmax
        scal_ref[1] = 2.0 * reach + (bmax - bmin)

    rows = pl.ds(pl.multiple_of(qi * tq, tq), tq)
    q0 = q0_ref[rows, :]
    q1 = q1_ref[rows, :]
    shift = scal_ref[0]
    slack = scal_ref[1]
    c0 = pl.multiple_of(s - (qi + 1) * tq, tq)
    lq = lamqk_ref[...]
    lam = (jnp.exp(jnp.sum(lq[0:1] * lq[1:2], axis=-1, keepdims=True))
           - jnp.exp(jnp.sum(lq[2:3] * lq[3:4], axis=-1, keepdims=True)) + lam_init)

    def logits_of(qm):
        return lax.dot_general(qm, kn_ref[...], (((1,), (1,)), ((), ())),
                               preferred_element_type=F32) + bias_ref[:, pl.ds(c0, s)]

    def finish(out):
        o_ref[...] = (_rms(out, gh_ref[...]) * (1.0 - lam_init)).astype(o_ref.dtype)

    @pl.when(slack <= SOFTMAX_SHIFT_SLACK)
    def _():
        def weighted_v(qm):
            p = jnp.exp(logits_of(qm) - shift).astype(BF16)
            return jnp.dot(p, vx_ref[...], preferred_element_type=F32)

        o0 = weighted_v(q0)
        o1 = weighted_v(q1)
        finish(o0[:, :LANES] / o0[:, LANES:LANES + 1] - o1[:, :LANES] * (lam / o1[:, LANES:LANES + 1]))

    @pl.when(slack > SOFTMAX_SHIFT_SLACK)
    def _():
        def softmax_map(qm):
            logits = logits_of(qm)
            p = jnp.exp(logits - jnp.max(logits, axis=-1, keepdims=True))
            return p, jnp.sum(p, axis=-1, keepdims=True)

        p0, l0 = softmax_map(q0)
        p1, l1 = softmax_map(q1)
        attn = p0 * (1.0 / l0) - p1 * (lam / l1)
        finish(jnp.dot(attn.astype(BF16), v_ref[...], preferred_element_type=F32))


def _diff_attention(proj3, bias_win, relb_t, gqk, lamqk, gh, tq, lam_init):
    b, s, _ = proj3.shape
    width = bias_win.shape[-1]
    return pl.pallas_call(
        functools.partial(_attn_kernel, tq=tq, lam_init=lam_init),
        grid=(ATTN_HEADS, b, s // tq),
        in_specs=[
            pl.BlockSpec(memory_space=pltpu.SMEM),
            pl.BlockSpec((None, s, LANES), lambda h, i, q: (i, 0, COL_Q // LANES + h)),
            pl.BlockSpec((None, s, LANES), lambda h, i, q: (i, 0, COL_K // LANES + h)),
            pl.BlockSpec((None, s, LANES), lambda h, i, q: (i, 0, COL_V // LANES + h)),
            pl.BlockSpec((None, tq, width), lambda h, i, q: (h, 0, 0)),
            pl.BlockSpec((2, LANES), lambda h, i, q: (0, 0)),
            pl.BlockSpec((4, ATTN_HEAD_DIM), lambda h, i, q: (0, 0)),
            pl.BlockSpec((1, LANES), lambda h, i, q: (0, 0)),
        ],
        out_specs=pl.BlockSpec((None, tq, LANES), lambda h, i, q: (i, q, h)),
        out_shape=jax.ShapeDtypeStruct((b, s, ATTN_WIDTH), BF16),
        scratch_shapes=[pltpu.VMEM((s, LANES), BF16), pltpu.VMEM((s, LANES), BF16), pltpu.VMEM((s, LANES), BF16),
                        pltpu.VMEM((s, 2 * LANES), BF16), pltpu.SMEM((2,), F32)],
        compiler_params=_params("parallel", "parallel", "arbitrary"),
        name="diff_attention",
    )(relb_t, proj3, proj3, proj3, bias_win, gqk, lamqk, gh)


def _ssd_kernel(x_ref, b_ref, c_ref, ac_ref, act_ref, dtt_ref, dsk_ref, y_ref, st_ref, acc_ref, bt_ref):
    L = SSM_CHUNK
    s = x_ref.shape[0]
    n_chunks = s // L
    row = lax.broadcasted_iota(jnp.int32, (L, L), 0)
    col = lax.broadcasted_iota(jnp.int32, (L, L), 1)
    lo = lax.broadcasted_iota(jnp.int32, (L, LANES), 1) < SSM_HEAD_DIM
    lo1 = lo[0:1, :]
    n_pairs = SSM_HEADS_PER_GROUP // 2

    bt_ref[...] = b_ref[...].astype(F32).T.astype(BF16)

    def one_chunk(c, backward):
        d = 1 if backward else 0
        rows = pl.ds(pl.multiple_of(c * L, L), L)
        off = SSM_HEADS_PER_GROUP * d
        tot_row = 0 if backward else L - 1
        mask = (row <= col) if backward else (row >= col)
        xb = x_ref[rows, :]
        cc = c_ref[rows, :]
        acc = ac_ref[rows, :]
        act = act_ref[:, rows]
        dtt = dtt_ref[:, rows]
        bt = bt_ref[:, rows].astype(F32)
        cb = lax.dot_general(cc, b_ref[rows, :], (((1,), (1,)), ((), ())), preferred_element_type=F32)

        ys = []
        for j in range(n_pairs):
            lanes = slice(j * LANES, (j + 1) * LANES)
            x_pair = xb[:, lanes]
            y_off = jnp.dot(cc, st_ref[d, :, lanes].astype(BF16), preferred_element_type=F32)
            diag, stn, acol, tot = [], [], [], []
            for e in (2 * j, 2 * j + 1):
                a_col = jnp.broadcast_to(acc[:, off + e:off + e + 1], (L, LANES))
                a_row = act[off + e:off + e + 1, :]
                dt_row = dtt[off + e:off + e + 1, :]
                tot_e = acc[tot_row:tot_row + 1, off + e:off + e + 1]
                decay = jnp.where(mask, jnp.exp2(a_col - a_row), 0.0)
                diag.append(jnp.dot((cb * decay * dt_row).astype(BF16), x_pair, preferred_element_type=F32))
                w_row = dt_row * jnp.exp2(tot_e - a_row)
                stn.append(jnp.dot((bt * w_row).astype(BF16), x_pair, preferred_element_type=F32))
                acol.append(a_col)
                tot.append(tot_e)
            ys.append(jnp.where(lo, diag[0], diag[1]) + y_off * jnp.exp2(jnp.where(lo, acol[0], acol[1])))
            st_ref[d, :, lanes] = (st_ref[d, :, lanes] * jnp.exp2(jnp.where(lo1, tot[0], tot[1]))
                                   + jnp.where(lo, stn[0], stn[1]))
        y = jnp.concatenate(ys, axis=1)
        if not backward:
            y = y + xb.astype(F32) * dsk_ref[...]
        acc_ref[d, rows, :] = y

    st_ref[...] = jnp.zeros_like(st_ref)

    def body(c, carry):
        one_chunk(c, False)
        one_chunk(n_chunks - 1 - c, True)
        return carry

    lax.fori_loop(0, n_chunks, body, 0)
    y_ref[...] = (acc_ref[0] + acc_ref[1]).astype(y_ref.dtype)


def _ssd_scan(u, ac_g, act_g, dtt_g, dskip):
    b, s, _ = u.shape
    gw = SSM_HEADS_PER_GROUP * SSM_HEAD_DIM
    xcols = SSM_INNER // SSM_STATE
    nl = DT_LANES_PER_GROUP
    return pl.pallas_call(
        _ssd_kernel,
        grid=(b, SSM_GROUPS),
        in_specs=[
            pl.BlockSpec((None, s, gw), lambda i, g: (i, 0, g)),
            pl.BlockSpec((None, s, SSM_STATE), lambda i, g: (i, 0, xcols + g)),
            pl.BlockSpec((None, s, SSM_STATE), lambda i, g: (i, 0, xcols + SSM_GROUPS + g)),
            pl.BlockSpec((None, s, LANES), lambda i, g: (i, 0, g)),
            pl.BlockSpec((None, nl, s), lambda i, g: (i, g, 0)),
            pl.BlockSpec((None, nl, s), lambda i, g: (i, g, 0)),
            pl.BlockSpec((1, gw), lambda i, g: (0, g)),
        ],
        out_specs=pl.BlockSpec((None, s, gw), lambda i, g: (i, 0, g)),
        out_shape=jax.ShapeDtypeStruct((b, s, SSM_INNER), BF16),
        scratch_shapes=[pltpu.VMEM((2, SSM_STATE, gw), F32), pltpu.VMEM((2, s, gw), F32),
                        pltpu.VMEM((SSM_STATE, s), BF16)],
        compiler_params=_params("parallel", "parallel"),
        name="ssd_scan",
    )(u, u, u, ac_g, act_g, dtt_g, dskip)


def _route(logits):
    lane = lax.broadcasted_iota(jnp.int32, logits.shape, 1)
    lane_f = lane.astype(F32)
    neg = -jnp.inf
    big = float(LANES)
    is_g = lane < MOE_GROUPS
    gl = jnp.where(is_g, logits, neg)
    gmax = jnp.max(gl, axis=-1, keepdims=True)
    g_idx = jnp.min(jnp.where(gl == gmax, lane_f, big), axis=-1, keepdims=True)
    g_gate = 1.0 / jnp.sum(jnp.where(is_g, jnp.exp(logits - gmax), 0.0), axis=-1, keepdims=True)
    first = MOE_GROUPS + MOE_EXPERTS_PER_GROUP * g_idx
    in_grp = (lane_f >= first) & (lane_f < first + MOE_EXPERTS_PER_GROUP)
    el = jnp.where(in_grp, logits, neg)
    v1 = jnp.max(el, axis=-1, keepdims=True)
    i1 = jnp.min(jnp.where(el == v1, lane_f, big), axis=-1, keepdims=True)
    el2 = jnp.where(lane_f == i1, neg, el)
    v2 = jnp.max(el2, axis=-1, keepdims=True)
    i2 = jnp.min(jnp.where(el2 == v2, lane_f, big), axis=-1, keepdims=True)
    r = jnp.exp(v2 - v1)
    p1 = 1.0 / (1.0 + r)
    p2 = r * p1
    ids = jnp.where(lane == 0, i1 - MOE_GROUPS, jnp.where(lane == 1, i2 - MOE_GROUPS, 0.0)).astype(jnp.int32)
    gates = jnp.where(lane == 0, p1 * g_gate, jnp.where(lane == 1, p2 * g_gate, 0.0))
    return ids, gates


_HI_HALF = 0xFFFF0000


def _pack_bf16_pairs(x):
    n = x.shape[1] // 2
    bits = lax.bitcast_convert_type(x.astype(BF16).astype(F32), jnp.uint32)
    return (bits[:, n:] & jnp.uint32(_HI_HALF)) | (bits[:, :n] >> 16)


def _unpack_bf16_pairs(w):
    lo = lax.bitcast_convert_type(w << 16, F32)
    hi = lax.bitcast_convert_type(w & jnp.uint32(_HI_HALF), F32)
    return jnp.concatenate([lo, hi], axis=1)


def _rank_assignments(ids, carry):
    tt = ids.shape[0]
    lane = lax.broadcasted_iota(jnp.int32, ids.shape, 1)
    oh0 = jnp.where(lane == ids[:, 0:1], 1.0, 0.0)
    oh1 = jnp.where(lane == ids[:, 1:2], 1.0, 0.0)
    earlier = jnp.where(lax.broadcasted_iota(jnp.int32, (tt, tt), 1) < lax.broadcasted_iota(jnp.int32, (tt, tt), 0),
                        1.0, 0.0).astype(BF16)
    before0 = jnp.dot(earlier, oh0.astype(BF16), preferred_element_type=F32)
    before1 = jnp.dot(earlier, oh1.astype(BF16), preferred_element_type=F32)
    tot0 = jnp.sum(oh0, axis=0, keepdims=True)
    rank0 = jnp.sum(oh0 * (before0 + carry), axis=-1, keepdims=True)
    rank1 = jnp.sum(oh1 * (before1 + tot0 + carry), axis=-1, keepdims=True)
    rank = jnp.where(lane == 0, rank0, jnp.where(lane == 1, rank1, 0.0)).astype(jnp.int32)
    return rank, carry + tot0 + jnp.sum(oh1, axis=0, keepdims=True)


def _post_kernel(attn_ref, y_ref, z0_ref, z1_ref, ga_ref, gm_ref, x_ref, gs_ref, wa_ref, ws_ref, wo_ref, g2_ref,
                 wr_ref, br_ref, xo_ref, hn_ref, id_ref, gate_ref, rank_ref, cnt_ref, carry_ref):
    @pl.when(pl.program_id(0) == 0)
    def _():
        carry_ref[...] = jnp.zeros_like(carry_ref)

    ya = jnp.dot(attn_ref[...], wa_ref[...], preferred_element_type=F32)
    z = jnp.concatenate([z0_ref[...], z1_ref[...]], axis=1).astype(F32)
    gated = y_ref[...].astype(F32) * _silu(z)
    ym = jnp.dot(_rms(gated, gs_ref[...]).astype(BF16), ws_ref[...], preferred_element_type=F32)
    merged = (_sigmoid(ga_ref[...].astype(F32)) * ya + _sigmoid(gm_ref[...].astype(F32)) * ym)
    x_new = x_ref[...] + jnp.dot(merged.astype(BF16), wo_ref[...], preferred_element_type=F32)
    xo_ref[...] = x_new
    hn = _rms(x_new, g2_ref[...])
    hn_ref[:, 0, :] = _pack_bf16_pairs(hn)
    hn_hi = hn.astype(BF16)
    hn_lo = (hn - hn_hi.astype(F32)).astype(BF16)
    wr = wr_ref[...]
    wr_hi = wr.astype(BF16)
    wr_lo = (wr - wr_hi.astype(F32)).astype(BF16)
    logits = (jnp.dot(hn_hi, wr_hi, preferred_element_type=F32) + jnp.dot(hn_lo, wr_hi, preferred_element_type=F32)
              + jnp.dot(hn_hi, wr_lo, preferred_element_type=F32) + br_ref[...])
    ids, gates = _route(logits)
    id_ref[...] = ids
    gate_ref[...] = gates
    rank, carry = _rank_assignments(ids, carry_ref[...])
    rank_ref[...] = rank
    carry_ref[...] = carry
    cnt_ref[...] = carry


def _post_mixer(attn, y, proj, x, gs, wa, ws, wo, g2, wr, br, tm):
    t, d = x.shape
    row = lambda w, c: pl.BlockSpec((tm, w), lambda i: (i, c))
    full = lambda a: pl.BlockSpec(a.shape, lambda i: (0,) * a.ndim)
    return pl.pallas_call(
        _post_kernel,
        grid=(t // tm,),
        in_specs=[
            row(ATTN_WIDTH, 0),
            row(SSM_INNER, 0),
            row(D_MODEL, COL_Z // D_MODEL),
            row(D_MODEL, COL_Z // D_MODEL + 1),
            row(D_MODEL, COL_GA // D_MODEL),
            row(D_MODEL, COL_GM // D_MODEL),
            row(D_MODEL, 0),
            full(gs), full(wa), full(ws), full(wo), full(g2), full(wr), full(br),
        ],
        out_specs=[row(D_MODEL, 0), pl.BlockSpec((tm, 1, PACKED), lambda i: (i, 0, 0)), row(LANES, 0), row(LANES, 0),
                   row(LANES, 0), pl.BlockSpec((1, LANES), lambda i: (0, 0))],
        out_shape=[
            jax.ShapeDtypeStruct((t, d), F32),
            jax.ShapeDtypeStruct((t, 1, PACKED), jnp.uint32),
            jax.ShapeDtypeStruct((t, LANES), jnp.int32),
            jax.ShapeDtypeStruct((t, LANES), F32),
            jax.ShapeDtypeStruct((t, LANES), jnp.int32),
            jax.ShapeDtypeStruct((1, LANES), F32),
        ],
        scratch_shapes=[pltpu.VMEM((1, LANES), F32)],
        compiler_params=_params("arbitrary"),
        name="post_mixer",
    )(attn, y, proj, proj, proj, proj, x, gs, wa, ws, wo, g2, wr, br)


GATHER_DEPTH = 3


def _start_rows(idx_ref, base, src_hbm, dst_ref, sem, rows=None, priority=None):
    for r in (range(dst_ref.shape[0]) if rows is None else rows):
        pltpu.make_async_copy(src_hbm.at[idx_ref[base + r]], dst_ref.at[pl.ds(r, 1), :], sem).start(
            priority=r % 2 if priority is None else priority)


def _wait_buffer(dst_ref, sem):
    pltpu.make_async_copy(dst_ref, dst_ref, sem).wait()


def _moe_kernel(blk_e_ref, pos_ref, nused_ref, seg_ref, hn_hbm, wg_ref, wu_ref, wd_ref, y_ref, src_ref, xg_ref,
                wgb_ref, wub_ref, wdb_ref, sem, *, n_tokens):
    i = pl.program_id(0)
    nused = nused_ref[0]
    quarter = EXPERT_ROWS // 4

    def start(blk, part=None):
        slot = blk % GATHER_DEPTH
        rows = range(EXPERT_ROWS) if part is None else range(part * quarter, (part + 1) * quarter)
        _start_rows(src_ref, blk * EXPERT_ROWS, hn_hbm, xg_ref.at[slot], sem.at[slot], rows, priority=1)

    @pl.when(i == 0)
    def _():
        def clear(j, carry):
            src_ref[j] = 0
            return carry

        for e in range(MOE_EXPERTS):
            lax.fori_loop(seg_ref[e], seg_ref[MOE_EXPERTS + e], clear, 0)
        for k in range(MOE_TOPK):
            def fill(tok, carry):
                src_ref[pos_ref[k * n_tokens + tok]] = tok
                return carry

            lax.fori_loop(0, n_tokens, fill, 0, unroll=8)
        start(0)

    @pl.when((i == 0) & (nused > 1))
    def _():
        start(1)

    new_expert = (i == 0) | (blk_e_ref[i] != blk_e_ref[jnp.maximum(i - 1, 0)])

    @pl.when((i < nused) & new_expert)
    def _():
        wgb_ref[...] = wg_ref[...].astype(BF16)
        wub_ref[...] = wu_ref[...].astype(BF16)
        wdb_ref[...] = wd_ref[...].astype(BF16)

    def compute(prefetch):
        xb = _unpack_bf16_pairs(xg_ref[i % GATHER_DEPTH]).astype(BF16)
        prefetch(0)
        hg = jnp.dot(xb, wgb_ref[...], preferred_element_type=F32)
        prefetch(1)
        hu = jnp.dot(xb, wub_ref[...], preferred_element_type=F32)
        prefetch(2)
        hid = (_silu(hg) * hu).astype(BF16)
        y = jnp.dot(hid, wdb_ref[...], preferred_element_type=F32)
        prefetch(3)
        y_ref[:, 0, :] = _pack_bf16_pairs(y)

    def wait():
        slot = i % GATHER_DEPTH
        _wait_buffer(xg_ref.at[slot], sem.at[slot])

    @pl.when(i + 2 < nused)
    def _():
        wait()
        compute(lambda part: start(i + 2, part))

    @pl.when((i < nused) & (i + 2 >= nused))
    def _():
        wait()
        compute(lambda part: None)

    @pl.when(i >= nused)
    def _():
        y_ref[...] = jnp.zeros_like(y_ref)


def _moe_experts(blk_expert, pos_kt, nused, seg, hn3, wg, wu, wd, layer):
    n_blocks = blk_expert.shape[0]
    d = wg.shape[2]
    n_tokens = hn3.shape[0]
    grid_spec = pltpu.PrefetchScalarGridSpec(
        num_scalar_prefetch=4,
        grid=(n_blocks,),
        in_specs=[
            pl.BlockSpec(memory_space=pl.ANY),
            pl.BlockSpec((None, None, d, MOE_FF), lambda i, be, st, nu, sg: (layer, be[i], 0, 0)),
            pl.BlockSpec((None, None, d, MOE_FF), lambda i, be, st, nu, sg: (layer, be[i], 0, 0)),
            pl.BlockSpec((None, None, MOE_FF, d), lambda i, be, st, nu, sg: (layer, be[i], 0, 0)),
        ],
        out_specs=pl.BlockSpec((EXPERT_ROWS, 1, PACKED), lambda i, be, st, nu, sg: (i, 0, 0)),
        scratch_shapes=[
            pltpu.SMEM((n_blocks * EXPERT_ROWS,), jnp.int32),
            pltpu.VMEM((GATHER_DEPTH, EXPERT_ROWS, PACKED), jnp.uint32),
            pltpu.VMEM((d, MOE_FF), BF16),
            pltpu.VMEM((d, MOE_FF), BF16),
            pltpu.VMEM((MOE_FF, d), BF16),
            pltpu.SemaphoreType.DMA((GATHER_DEPTH,)),
        ],
    )
    return pl.pallas_call(
        functools.partial(_moe_kernel, n_tokens=n_tokens),
        grid_spec=grid_spec,
        out_shape=jax.ShapeDtypeStruct((n_blocks * EXPERT_ROWS, 1, PACKED), jnp.uint32),
        compiler_params=_params("arbitrary"),
        name="moe_experts",
    )(blk_expert, pos_kt, nused, seg, hn3, wg, wu, wd)


def _combine_kernel(pos_ref, x_ref, gate_ref, y_hbm, o_ref, yg_ref, sem, *, tt, n_tiles):
    i = pl.program_id(0)

    def start(tile):
        slot = tile % GATHER_DEPTH
        for k in range(MOE_TOPK):
            _start_rows(pos_ref, (k * n_tiles + tile) * tt, y_hbm, yg_ref.at[slot, k], sem.at[slot])

    def finish():
        gates = gate_ref[...]
        slot = i % GATHER_DEPTH
        o_ref[...] = (x_ref[...] + gates[:, 0:1] * _unpack_bf16_pairs(yg_ref[slot, 0])
                      + gates[:, 1:2] * _unpack_bf16_pairs(yg_ref[slot, 1]))

    def wait():
        slot = i % GATHER_DEPTH
        _wait_buffer(yg_ref.at[slot], sem.at[slot])

    @pl.when(i == 0)
    def _():
        start(0)
        if n_tiles > 1:
            start(1)

    @pl.when(i + 2 < n_tiles)
    def _():
        wait()
        start(i + 2)
        finish()

    @pl.when(i + 2 >= n_tiles)
    def _():
        wait()
        finish()


def _moe_combine(pos_kt, x, gates, y3, tt):
    t, d = x.shape
    n_tiles = t // tt
    grid_spec = pltpu.PrefetchScalarGridSpec(
        num_scalar_prefetch=1,
        grid=(n_tiles,),
        in_specs=[
            pl.BlockSpec((tt, d), lambda i, p: (i, 0)),
            pl.BlockSpec((tt, LANES), lambda i, p: (i, 0)),
            pl.BlockSpec(memory_space=pl.ANY),
        ],
        out_specs=pl.BlockSpec((tt, d), lambda i, p: (i, 0)),
        scratch_shapes=[pltpu.VMEM((GATHER_DEPTH, MOE_TOPK, tt, PACKED), jnp.uint32),
                        pltpu.SemaphoreType.DMA((GATHER_DEPTH,))],
    )
    return pl.pallas_call(
        functools.partial(_combine_kernel, tt=tt, n_tiles=n_tiles),
        grid_spec=grid_spec,
        out_shape=jax.ShapeDtypeStruct((t, d), F32),
        compiler_params=_params("arbitrary"),
        name="moe_combine",
    )(pos_kt, x, gates, y3)


def _dispatch_plan(ids, rank, cnt):
    t = ids.shape[0]
    n_assign = t * MOE_TOPK
    counts = cnt[0, :MOE_EXPERTS].astype(jnp.int32)
    padded = (counts + EXPERT_ROWS - 1) // EXPERT_ROWS * EXPERT_ROWS
    pad_end = jnp.cumsum(padded)
    pad_start = pad_end - padded
    e_ids = ids[:, :MOE_TOPK]
    onehot = e_ids[:, :, None] == jnp.arange(MOE_EXPERTS, dtype=jnp.int32)[None, None, :]
    dest = jnp.sum(jnp.where(onehot, pad_start[None, None, :], 0), axis=-1) + rank[:, :MOE_TOPK]
    n_blocks = -(-n_assign // EXPERT_ROWS) + MOE_EXPERTS
    blk_row0 = jnp.arange(n_blocks, dtype=jnp.int32) * EXPERT_ROWS
    blk_expert = jnp.minimum(jnp.sum((pad_end[None, :] <= blk_row0[:, None]).astype(jnp.int32), axis=1),
                             MOE_EXPERTS - 1)
    nused = (pad_end[-1:] // EXPERT_ROWS).astype(jnp.int32)
    pos_kt = dest.T.reshape(-1).astype(jnp.int32)
    seg = jnp.concatenate([pad_start + counts, pad_end]).astype(jnp.int32)
    return blk_expert, nused, pos_kt, seg


def _pick(n, pref):
    return pref if n % pref == 0 else n


def kernel(x, norm1_g, w_in, qk_norm_g, lambda_qk, attn_head_norm_g, rel_bias, conv_w, conv_b, dt_bias, a_log,
           d_skip, ssm_norm_g, w_attn_out, w_ssm_out, w_out, norm2_g, w_router_group, b_router_group,
           w_router_expert, b_router_expert, w_exp_gate, w_exp_up, w_exp_down):
    b, s, d = x.shape
    t = b * s
    tq = _pick(s, 512)
    bias_win = _bias_windows(rel_bias, s, tq)

    lane = np.arange(DT_COLS)
    grp, rem = lane // DT_LANES_PER_GROUP, lane % DT_LANES_PER_GROUP
    direction, e = rem // SSM_HEADS_PER_GROUP, rem % SSM_HEADS_PER_GROUP
    dt_src = direction * SSM_HEADS + grp * SSM_HEADS_PER_GROUP + e
    pad_lanes = LANES - DT_COLS

    xf = x.reshape(t, d)
    for i in range(DEPTH):
        lam_init = 0.8 - 0.6 * math.exp(-0.3 * i)
        g1 = norm1_g[i][None, :]
        w_tail = lax.slice(w_in, (i, 0, MAIN_COLS), (i + 1, d, w_in.shape[2]))
        w_dt = jnp.pad(w_tail[0, :, :DT_COLS][:, dt_src], ((0, 0), (0, pad_lanes))).astype(BF16)
        dt_b = jnp.pad(dt_bias[i].reshape(-1)[dt_src], (0, pad_lanes))[None, :]
        a_neg = jnp.pad(-jnp.exp(a_log[i].astype(F32)).reshape(-1)[dt_src], (0, pad_lanes))[None, :]

        proj, dt_raw = _norm_matmul(xf, g1, w_in, i, w_tail[0, :, DT_COLS:], w_dt, BF16, _pick(t, 2048), 1024)
        proj3 = proj.reshape(b, s, PROJ_COLS)
        ac, act, dtt = _dt_prep(dt_raw.reshape(b, s, LANES), dt_b, a_neg)

        u = _conv_silu(proj3, conv_w[i], conv_b[i][None, :], 256)
        y_ssd = _ssd_scan(u, ac, act, dtt, jnp.repeat(d_skip[i], SSM_HEAD_DIM)[None, :])

        gqk = jnp.tile(qk_norm_g[i], (1, 2))
        attn = _diff_attention(proj3, bias_win, rel_bias.T.astype(F32), gqk, lambda_qk[i],
                               attn_head_norm_g[i][None, :], tq, lam_init)

        w_r = jnp.pad(jnp.concatenate([w_router_group[i], w_router_expert[i]], axis=1),
                      ((0, 0), (0, LANES - MOE_GROUPS - MOE_EXPERTS)))
        b_r = jnp.pad(jnp.concatenate([b_router_group[i], b_router_expert[i]]),
                      (0, LANES - MOE_GROUPS - MOE_EXPERTS))[None, :]
        x_mid, hn, ids, gates, rank, cnt = _post_mixer(
            attn.reshape(t, ATTN_WIDTH), y_ssd.reshape(t, SSM_INNER), proj, xf, ssm_norm_g[i][None, :],
            w_attn_out[i].astype(BF16), w_ssm_out[i].astype(BF16), w_out[i].astype(BF16), norm2_g[i][None, :],
            w_r, b_r, _pick(t, 256))

        blk_expert, nused, pos_kt, seg = _dispatch_plan(ids, rank, cnt)
        y_pad = _moe_experts(blk_expert, pos_kt, nused, seg, hn, w_exp_gate, w_exp_up, w_exp_down, i)
        xf = _moe_combine(pos_kt, x_mid, gates, y_pad, _pick(t, 512))
    return xf.reshape(b, s, d)
```
